```python
import math
import jax, jax.numpy as jnp
from jax import lax
import numpy as np

D_MODEL = 2048
BATCH = 4
SEQ = 2048
DEPTH = 4

GRID_W = 64
CTX_LEN = 256

GLA_HEADS = 4
GLA_DK = 128
GLA_DV = 256
GLA_QK = GLA_HEADS * GLA_DK
GLA_V = GLA_HEADS * GLA_DV
GLA_GATE_RANK = 16
GLA_GATE_TAU = 16.0
GLA_CHUNK = 64
S5_WIDTH = 768
S5_GROUP = 16
S5_GROUPS = S5_WIDTH // S5_GROUP
S5_STATE = 64
ATTN_Q_HEADS = 8
ATTN_KV_HEADS = 2
ATTN_HEAD_DIM = 128
ATTN_Q = ATTN_Q_HEADS * ATTN_HEAD_DIM
ATTN_KV = ATTN_KV_HEADS * ATTN_HEAD_DIM
ATTN_BLOCK = 128
ROPE_THETA = 10000.0
D_FF = ((8 * D_MODEL + 3 * 256 - 1) // (3 * 256)) * 256
IN_SPLITS = (GLA_QK, GLA_QK, GLA_V, GLA_V, GLA_GATE_RANK, S5_WIDTH, ATTN_Q, ATTN_KV, ATTN_KV, 3 * D_MODEL)
IN_WIDTH = 2 * GLA_QK + 2 * GLA_V + GLA_GATE_RANK + S5_WIDTH + ATTN_Q + 2 * ATTN_KV + 3 * D_MODEL
DN_ALPHA = (2 * DEPTH) ** 0.25
DN_BETA = (8 * DEPTH) ** -0.25
EPS = 1e-6

kernel_name = "hybrid_gla_s5_gqa_prefix_dit"

F32 = jnp.float32


def layer_norm(x, w, b):
    xf = x.astype(F32)
    mu = jnp.mean(xf, -1, keepdims=True)
    var = jnp.mean(jnp.square(xf - mu), -1, keepdims=True)
    return ((xf - mu) * lax.rsqrt(var + EPS) * w + b).astype(x.dtype)


def rms_norm(x, w):
    xf = x.astype(F32)
    return (xf * lax.rsqrt(jnp.mean(xf * xf, -1, keepdims=True) + EPS) * w).astype(x.dtype)


def modulate(h, shift, scale):
    return h * (1 + scale) + shift


def post_norm(x, y, w, b):
    return layer_norm(DN_ALPHA * x + y, w, b)


def axial_rope(n_rows):
    rows = jnp.repeat(jnp.arange(n_rows), GRID_W).astype(F32)
    cols = jnp.tile(jnp.arange(GRID_W), n_rows).astype(F32)
    n_freq = ATTN_HEAD_DIM // 4
    inv = ROPE_THETA ** (-jnp.arange(n_freq, dtype=F32) / n_freq)
    ang = jnp.concatenate([rows[:, None] * inv, cols[:, None] * inv], -1)
    return jnp.cos(ang)[:, None, :], jnp.sin(ang)[:, None, :]


def apply_rope(x, cos, sin):
    xf = x.astype(F32).reshape(x.shape[:-1] + (x.shape[-1] // 2, 2))
    x0, x1 = xf[..., 0], xf[..., 1]
    out = jnp.stack([x0 * cos - x1 * sin, x0 * sin + x1 * cos], -1)
    return out.reshape(x.shape).astype(x.dtype)


def attend(q, k, v):
    s = jnp.einsum('bqhgd,bkhd->bhgqk', q, k, preferred_element_type=F32) * (ATTN_HEAD_DIM ** -0.5)
    p = jax.nn.softmax(s, axis=-1).astype(v.dtype)
    return jnp.einsum('bhgqk,bkhd->bqhgd', p, v)


def gqa_blocks(q, k, v):
    bsz, t = q.shape[:2]
    grp = ATTN_Q_HEADS // ATTN_KV_HEADS
    qb = q.reshape(bsz, t // ATTN_BLOCK, ATTN_BLOCK, ATTN_KV_HEADS, grp, ATTN_HEAD_DIM).swapaxes(0, 1)
    ob = lax.map(lambda qq: attend(qq, k, v), qb)
    return ob.swapaxes(0, 1).reshape(bsz, t, ATTN_Q)


def gla_chunked(q, k, v, log_a, s0):
    bsz, t, nh, _ = q.shape
    dv = v.shape[-1]
    n = t // GLA_CHUNK
    mask = jnp.tril(jnp.ones((GLA_CHUNK, GLA_CHUNK), bool))

    def to_chunks(z):
        return z.reshape(bsz, n, GLA_CHUNK, nh, z.shape[-1]).swapaxes(0, 1)

    def step(s, inp):
        qc, kc, vc, ac = inp
        b = jnp.cumsum(ac, axis=1)
        qe = qc * jnp.exp(b)
        ke = kc * jnp.exp(-b)
        att = jnp.where(mask, jnp.einsum('bihd,bjhd->bhij', qe, ke), 0.0)
        o = jnp.einsum('bhij,bjhv->bihv', att, vc) + jnp.einsum('bihd,bhdv->bihv', qe, s)
        b_last = b[:, -1]
        kd = kc * jnp.exp(b_last[:, None] - b)
        s = jnp.exp(b_last)[..., None] * s + jnp.einsum('bjhd,bjhv->bhdv', kd, vc)
        return s, o

    s_fin, o = lax.scan(step, s0, (to_chunks(q), to_chunks(k), to_chunks(v), to_chunks(log_a)))
    return o.swapaxes(0, 1).reshape(bsz, t, nh, dv), s_fin


def gla_bidirectional(q, k, v, glr, w_gate, b_gate, n_ctx):
    q, k, v = q.astype(F32), k.astype(F32), v.astype(F32)
    bsz, n_tok, nh, dk = q.shape
    dv = v.shape[-1]
    outs = []
    for d in range(2):
        log_a = jax.nn.log_sigmoid((glr @ w_gate[d] + b_gate[d]).astype(F32)) / GLA_GATE_TAU
        log_a = log_a.reshape(bsz, n_tok, nh, dk)
        ctx_in = [z[:, :n_ctx] for z in (q, k, v, log_a)]
        lat_in = [z[:, n_ctx:] for z in (q, k, v, log_a)]
        if d == 1:
            ctx_in = [jnp.flip(z, 1) for z in ctx_in]
            lat_in = [jnp.flip(z, 1) for z in lat_in]
        s0 = jnp.zeros((bsz, nh, dk, dv), F32)
        o_c, s_c = gla_chunked(*ctx_in, s0)
        o_l, _ = gla_chunked(*lat_in, s_c)
        if d == 1:
            o_c, o_l = jnp.flip(o_c, 1), jnp.flip(o_l, 1)
        outs.append(jnp.concatenate([o_c, o_l], 1))
    return outs[0] + outs[1]


def s5_discretise(lam_re, lam_im, log_dt):
    lam_re, lam_im = lam_re.astype(F32), lam_im.astype(F32)
    dt = jnp.exp(log_dt.astype(F32))[:, None]
    mag = jnp.exp(lam_re * dt)
    a_re, a_im = mag * jnp.cos(lam_im * dt), mag * jnp.sin(lam_im * dt)
    den = lam_re * lam_re + lam_im * lam_im
    nr, ni = a_re - 1, a_im
    k_re = (nr * lam_re + ni * lam_im) / den
    k_im = (ni * lam_re - nr * lam_im) / den
    return a_re, a_im, k_re, k_im


def s5_scan(bu_re, bu_im, a_re, a_im, s0_re, s0_im, reverse):
    idx = -1 if reverse else 0
    bu_re = bu_re.at[:, idx].add(a_re * s0_re - a_im * s0_im)
    bu_im = bu_im.at[:, idx].add(a_re * s0_im + a_im * s0_re)
    t = bu_re.shape[1]
    ar = jnp.broadcast_to(a_re, (1, t) + a_re.shape)
    ai = jnp.broadcast_to(a_im, (1, t) + a_im.shape)

    def combine(e1, e2):
        a1r, a1i, b1r, b1i = e1
        a2r, a2i, b2r, b2i = e2
        return (a2r * a1r - a2i * a1i, a2r * a1i + a2i * a1r,
                a2r * b1r - a2i * b1i + b2r, a2r * b1i + a2i * b1r + b2i)

    _, _, xr, xi = lax.associative_scan(combine, (ar, ai, bu_re, bu_im), axis=1, reverse=reverse)
    return xr, xi


def s5_bidirectional(u, n_ctx, lam_re, lam_im, log_dt, b_re, b_im, c_re, c_im, d_skip):
    bsz, n_tok, _ = u.shape
    uf = u.astype(F32).reshape(bsz, n_tok, S5_GROUPS, S5_GROUP)
    bu_re = jnp.einsum('btgc,gpc->btgp', uf, b_re.astype(F32))
    bu_im = jnp.einsum('btgc,gpc->btgp', uf, b_im.astype(F32))
    y = uf * d_skip.astype(F32).reshape(S5_GROUPS, S5_GROUP)
    zeros = jnp.zeros((bsz, S5_GROUPS, S5_STATE), F32)
    for d in range(2):
        rev = d == 1
        a_re, a_im, k_re, k_im = s5_discretise(lam_re[d], lam_im[d], log_dt[d])
        xr = k_re * bu_re - k_im * bu_im
        xi = k_re * bu_im + k_im * bu_re
        cr, ci = s5_scan(xr[:, :n_ctx], xi[:, :n_ctx], a_re, a_im, zeros, zeros, rev)
        end = 0 if rev else -1
        lr, li = s5_scan(xr[:, n_ctx:], xi[:, n_ctx:], a_re, a_im, cr[:, end], ci[:, end], rev)
        sr = jnp.concatenate([cr, lr], 1)
        si = jnp.concatenate([ci, li], 1)
        y = y + jnp.einsum('btgp,gcp->btgc', sr, c_re[d].astype(F32)) \
              - jnp.einsum('btgp,gcp->btgc', si, c_im[d].astype(F32))
    return y.reshape(bsz, n_tok, S5_WIDTH)


def token_mixers(h, n_ctx, keep_ctx, cos, sin, w_in, w_gla_gate, b_gla_gate, gla_norm_w,
                 s5_lam_re, s5_lam_im, s5_log_dt, s5_b_re, s5_b_im, s5_c_re, s5_c_im, s5_d,
                 w_s5_glu, q_norm_w, k_norm_w, w_proj_gla, w_proj_s5, w_proj_attn, w_out):
    bsz, n_tok, _ = h.shape
    cuts = [int(s) for s in np.cumsum(IN_SPLITS)[:-1]]
    gq, gk, gv, gr, glr, su, aq, ak, av, bg = jnp.split(h @ w_in, cuts, axis=-1)
    lo = 0 if keep_ctx else n_ctx

    def heads(z, nh):
        return z.reshape(bsz, n_tok, nh, -1)

    o = gla_bidirectional(heads(gq, GLA_HEADS) * (GLA_DK ** -0.5), heads(gk, GLA_HEADS),
                          heads(gv, GLA_HEADS), glr, w_gla_gate, b_gla_gate, n_ctx)[:, lo:]
    mu = jnp.mean(o, -1, keepdims=True)
    var = jnp.mean(jnp.square(o - mu), -1, keepdims=True)
    o = (o - mu) * lax.rsqrt(var + EPS) * gla_norm_w.astype(F32).reshape(GLA_HEADS, GLA_DV)
    o_gla = o.reshape(bsz, n_tok - lo, GLA_V).astype(h.dtype) * jax.nn.silu(gr[:, lo:])

    y = s5_bidirectional(su, n_ctx, s5_lam_re, s5_lam_im, s5_log_dt, s5_b_re, s5_b_im,
                         s5_c_re, s5_c_im, s5_d)[:, lo:].astype(h.dtype)
    y = jax.nn.gelu(y)
    o_s5 = y * jax.nn.sigmoid(y @ w_s5_glu)

    q = rms_norm(heads(aq, ATTN_Q_HEADS), q_norm_w)
    k = rms_norm(heads(ak, ATTN_KV_HEADS), k_norm_w)
    v = heads(av, ATTN_KV_HEADS)
    q_lat = apply_rope(q[:, n_ctx:], cos, sin)
    k_lat = apply_rope(k[:, n_ctx:], cos, sin)
    k_all = jnp.concatenate([k_lat, k[:, :n_ctx]], 1)
    v_all = jnp.concatenate([v[:, n_ctx:], v[:, :n_ctx]], 1)
    o_attn = gqa_blocks(q_lat, k_all, v_all)
    if keep_ctx:
        grp = ATTN_Q_HEADS // ATTN_KV_HEADS
        q_c = q[:, :n_ctx].reshape(bsz, n_ctx, ATTN_KV_HEADS, grp, ATTN_HEAD_DIM)
        o_c = attend(q_c, k[:, :n_ctx], v[:, :n_ctx]).reshape(bsz, n_ctx, ATTN_Q)
        o_attn = jnp.concatenate([o_c, o_attn], 1)

    g_a, g_b, g_c = jnp.split(jax.nn.sigmoid(bg[:, lo:]), 3, axis=-1)
    merged = g_a * (o_gla @ w_proj_gla) + g_b * (o_s5 @ w_proj_s5) + g_c * (o_attn @ w_proj_attn)
    return merged @ w_out


def swiglu(h, w_ffn_in, w_ffn_out):
    a, b = jnp.split(h @ w_ffn_in, 2, axis=-1)
    return (jax.nn.silu(a) * b) @ w_ffn_out


def setup_inputs(seed: int = 0) -> dict:
    key = jax.random.key(seed)
    keys = iter(jax.random.split(key, 40))

    def nrm(shape, std):
        return std * jax.random.normal(next(keys), shape, F32)

    L, D, G, P = DEPTH, D_MODEL, S5_GROUPS, S5_STATE
    lam_im = jnp.pi * jnp.arange(P, dtype=F32) + nrm((L, 2, G, P), 0.01)
    return {
        "x": nrm((BATCH, SEQ, D), 1.0),
        "c": nrm((BATCH, D), 1.0),
        "ctx": nrm((BATCH, CTX_LEN, D), 1.0),
        "c_ctx": nrm((D,), 1.0),
        "w_ada": nrm((L, D, 6 * D), 0.5 * D ** -0.5),
        "b_ada": nrm((L, 6 * D), 0.02),
        "w_in": nrm((L, D, IN_WIDTH), D ** -0.5),
        "w_gla_gate": nrm((L, 2, GLA_GATE_RANK, GLA_QK), GLA_GATE_RANK ** -0.5),
        "b_gla_gate": nrm((L, 2, GLA_QK), 0.1),
        "gla_norm_w": 1.0 + nrm((L, GLA_V), 0.02),
        "s5_lam_re": -0.5 + nrm((L, 2, G, P), 0.01),
        "s5_lam_im": lam_im,
        "s5_log_dt": jax.random.uniform(next(keys), (L, 2, G), F32, math.log(1e-3), math.log(1e-1)),
        "s5_b_re": nrm((L, G, P, S5_GROUP), (2 * S5_GROUP) ** -0.5),
        "s5_b_im": nrm((L, G, P, S5_GROUP), (2 * S5_GROUP) ** -0.5),
        "s5_c_re": nrm((L, 2, G, S5_GROUP, P), 0.5),
        "s5_c_im": nrm((L, 2, G, S5_GROUP, P), 0.5),
        "s5_d": nrm((L, S5_WIDTH), 1.0),
        "w_s5_glu": nrm((L, S5_WIDTH, S5_WIDTH), S5_WIDTH ** -0.5),
        "q_norm_w": 1.0 + nrm((L, ATTN_HEAD_DIM), 0.02),
        "k_norm_w": 1.0 + nrm((L, ATTN_HEAD_DIM), 0.02),
        "w_proj_gla": nrm((L, GLA_V, D), GLA_V ** -0.5),
        "w_proj_s5": nrm((L, S5_WIDTH, D), S5_WIDTH ** -0.5),
        "w_proj_attn": nrm((L, ATTN_Q, D), ATTN_Q ** -0.5),
        "w_out": nrm((L, D, D), DN_BETA * D ** -0.5),
        "ln1_w": 1.0 + nrm((L, D), 0.02),
        "ln1_b": nrm((L, D), 0.02),
        "ln2_w": 1.0 + nrm((L, D), 0.02),
        "ln2_b": nrm((L, D), 0.02),
        "w_ffn_in": nrm((L, D, 2 * D_FF), D ** -0.5),
        "w_ffn_out": nrm((L, D_FF, D), DN_BETA * D_FF ** -0.5),
    }


def reference(x, c, ctx, c_ctx, w_ada, b_ada, w_in, w_gla_gate, b_gla_gate, gla_norm_w,
              s5_lam_re, s5_lam_im, s5_log_dt, s5_b_re, s5_b_im, s5_c_re, s5_c_im, s5_d,
              w_s5_glu, q_norm_w, k_norm_w, w_proj_gla, w_proj_s5, w_proj_attn, w_out,
              ln1_w, ln1_b, ln2_w, ln2_b, w_ffn_in, w_ffn_out):
    n_ctx = ctx.shape[1]
    n_lat = x.shape[1]
    rows = n_lat // GRID_W
    cos, sin = axial_rope(rows)
    xc = ctx
    silu_c = jax.nn.silu(c)
    silu_cc = jax.nn.silu(c_ctx)
    for l in range(DEPTH):
        keep_ctx = l < DEPTH - 1
        mod = (silu_c @ w_ada[l] + b_ada[l])[:, None, :]
        mod_c = silu_cc @ w_ada[l] + b_ada[l]
        sh1, sc1, g1, sh2, sc2, g2 = jnp.split(mod, 6, axis=-1)
        csh1, csc1, cg1, csh2, csc2, cg2 = jnp.split(mod_c, 6, axis=-1)
        h = jnp.concatenate([modulate(xc, csh1, csc1), modulate(x, sh1, sc1)], axis=1)
        mix = token_mixers(h, n_ctx, keep_ctx, cos, sin, w_in[l], w_gla_gate[l], b_gla_gate[l],
                           gla_norm_w[l], s5_lam_re[l], s5_lam_im[l], s5_log_dt[l], s5_b_re[l],
                           s5_b_im[l], s5_c_re[l], s5_c_im[l], s5_d[l], w_s5_glu[l], q_norm_w[l],
                           k_norm_w[l], w_proj_gla[l], w_proj_s5[l], w_proj_attn[l], w_out[l])
        mix_lat = mix[:, mix.shape[1] - n_lat:]
        x = post_norm(x, g1 * mix_lat, ln1_w[l], ln1_b[l])
        x = post_norm(x, g2 * swiglu(modulate(x, sh2, sc2), w_ffn_in[l], w_ffn_out[l]), ln2_w[l], ln2_b[l])
        if keep_ctx:
            xc = post_norm(xc, cg1 * mix[:, :n_ctx], ln1_w[l], ln1_b[l])
            xc = post_norm(xc, cg2 * swiglu(modulate(xc, csh2, csc2), w_ffn_in[l], w_ffn_out[l]),
                           ln2_w[l], ln2_b[l])
    return x
```

```python
import functools

import jax
import jax.numpy as jnp
from jax import lax
from jax.experimental import pallas as pl
from jax.experimental.pallas import tpu as pltpu

F32 = jnp.float32
BF16 = jnp.bfloat16

GRID_W = 64
GLA_HEADS = 4
GLA_DK = 128
GLA_DV = 256
GLA_QK = GLA_HEADS * GLA_DK
GLA_V = GLA_HEADS * GLA_DV
GLA_GATE_RANK = 16
GLA_GATE_TAU = 16.0
GLA_CHUNK = 64
S5_WIDTH = 768
S5_GROUP = 16
S5_GROUPS = S5_WIDTH // S5_GROUP
S5_STATE = 64
ATTN_Q_HEADS = 8
ATTN_KV_HEADS = 2
ATTN_HEAD_DIM = 128
ATTN_Q = ATTN_Q_HEADS * ATTN_HEAD_DIM
ATTN_KV = ATTN_KV_HEADS * ATTN_HEAD_DIM
ATTN_GROUP = ATTN_Q_HEADS // ATTN_KV_HEADS
ROPE_THETA = 10000.0
EPS = 1e-6

S5_L = 16
S5_CW = S5_L * S5_GROUP
S5_P2 = 2 * S5_STATE
GLR_PAD = 256

LANE = 128
VMEM_LIMIT = 56 * 1024 * 1024


def _cparams(*sem):
    return pltpu.CompilerParams(dimension_semantics=sem, vmem_limit_bytes=VMEM_LIMIT)


def _pick(n, cands):
    for c in cands:
        if n % c == 0:
            return c
    raise ValueError(f"no tile for {n}")


def _dot(a, b):
    return jnp.dot(a, b, preferred_element_type=F32)


def _dot_nt(a, b):
    return lax.dot_general(a, b, (((1,), (1,)), ((), ())), preferred_element_type=F32)


def _dot_tn(a, b):
    return lax.dot_general(a, b, (((0,), (0,)), ((), ())), preferred_element_type=F32)


def _dot_hi(a, b):
    return jnp.dot(a, b, precision=lax.Precision.HIGHEST, preferred_element_type=F32)


def _head_lat(ref, i, tpb, nb):
    lat = ref[pl.ds(i // tpb, 1), :]
    head = jnp.where(i % tpb == 0, ref[nb:nb + 1, :], lat)
    return head, lat


def _ada_kernel(c_ref, w_ref, b_ref, o_ref):
    cc = c_ref[...]
    s = (cc * jax.nn.sigmoid(cc)).astype(BF16)
    o_ref[...] = _dot(s, w_ref[...].astype(BF16)) + b_ref[...]


def _ada_table(cc, w_ada, b_ada):
    depth, d, d6 = w_ada.shape
    tn = _pick(d6, (1024, 512, 256, 128))
    return pl.pallas_call(
        _ada_kernel,
        grid=(depth, d6 // tn),
        in_specs=[pl.BlockSpec((8, d), lambda l, j: (0, 0)),
                  pl.BlockSpec((None, d, tn), lambda l, j: (l, 0, j)),
                  pl.BlockSpec((None, 1, tn), lambda l, j: (l, 0, j))],
        out_specs=pl.BlockSpec((None, 8, tn), lambda l, j: (l, 0, j)),
        out_shape=jax.ShapeDtypeStruct((depth, 8, d6), F32),
        compiler_params=_cparams("parallel", "parallel"),
        name="ada_table",
    )(cc, w_ada, b_ada.reshape(depth, 1, d6))


def _modulate_to(h_ref, x_ref, sh_ref, sc_ref, i, tpb, nb, n_ctx):
    tm = x_ref.shape[0]
    sh_h, sh_l = _head_lat(sh_ref, i, tpb, nb)
    sc_h, sc_l = _head_lat(sc_ref, i, tpb, nb)
    if n_ctx > 0:
        h_ref[0:n_ctx, :] = (x_ref[0:n_ctx, :] * (1.0 + sc_h) + sh_h).astype(BF16)
    if n_ctx < tm:
        h_ref[n_ctx:tm, :] = (x_ref[n_ctx:tm, :] * (1.0 + sc_l) + sh_l).astype(BF16)


def _inproj_kernel(x_ref, sh_ref, sc_ref, w_ref, o_ref, h_ref, *, tpb, nb, n_ctx):
    i = pl.program_id(0)

    @pl.when(pl.program_id(1) == 0)
    def _():
        _modulate_to(h_ref, x_ref, sh_ref, sc_ref, i, tpb, nb, n_ctx)

    o_ref[...] = _dot(h_ref[...], w_ref[...]).astype(BF16)


def _in_proj(xs, mod_l, w, *, tm, tpb, nb, n_ctx):
    m, d = xs.shape
    wp = w.shape[1]
    tn = _pick(wp, (512, 256, 128))
    return pl.pallas_call(
        functools.partial(_inproj_kernel, tpb=tpb, nb=nb, n_ctx=n_ctx),
        grid=(m // tm, wp // tn),
        in_specs=[pl.BlockSpec((tm, d), lambda i, j: (i, 0)),
                  pl.BlockSpec((8, d), lambda i, j: (0, 0)),
                  pl.BlockSpec((8, d), lambda i, j: (0, 1)),
                  pl.BlockSpec((d, tn), lambda i, j: (0, j))],
        out_specs=pl.BlockSpec((tm, tn), lambda i, j: (i, j)),
        out_shape=jax.ShapeDtypeStruct((m, wp), BF16),
        scratch_shapes=[pltpu.VMEM((tm, d), BF16)],
        compiler_params=_cparams("parallel", "arbitrary"),
        name="in_proj",
    )(xs, mod_l, mod_l, w)


def _deepnorm_ln(o_ref, x_ref, g_ref, lw_ref, lb_ref, i, *, tpb, nb, n_ctx, alpha):
    tm = x_ref.shape[0]
    g_h, g_l = _head_lat(g_ref, i, tpb, nb)

    def seg(lo, hi, g):
        z = alpha * x_ref[lo:hi, :] + g * o_ref[lo:hi, :]
        mu = jnp.mean(z, -1, keepdims=True)
        zc = z - mu
        var = jnp.mean(zc * zc, -1, keepdims=True)
        o_ref[lo:hi, :] = zc * lax.rsqrt(var + EPS) * lw_ref[...] + lb_ref[...]

    if n_ctx > 0:
        seg(0, n_ctx, g_h)
    if n_ctx < tm:
        seg(n_ctx, tm, g_l)


def _ffn_kernel(x_ref, sh_ref, sc_ref, g_ref, lw_ref, lb_ref, wa_ref, wb_ref, wo_ref, o_ref, h_ref, *,
                nj, tpb, nb, n_ctx, alpha):
    i = pl.program_id(0)
    j = pl.program_id(1)

    @pl.when(j == 0)
    def _():
        _modulate_to(h_ref, x_ref, sh_ref, sc_ref, i, tpb, nb, n_ctx)
        o_ref[...] = jnp.zeros_like(o_ref)

    h = h_ref[...]
    a = _dot(h, wa_ref[...])
    b = _dot(h, wb_ref[...])
    act = (a * jax.nn.sigmoid(a) * b).astype(BF16)
    o_ref[...] += _dot(act, wo_ref[...])

    @pl.when(j == nj - 1)
    def _():
        _deepnorm_ln(o_ref, x_ref, g_ref, lw_ref, lb_ref, i, tpb=tpb, nb=nb, n_ctx=n_ctx, alpha=alpha)


def _ffn(xs, mod_l, w_in, w_out, ln_w, ln_b, *, tm, tpb, nb, n_ctx, alpha):
    m, d = xs.shape
    dff = w_out.shape[0]
    tn = _pick(dff, (512, 256, 128))
    nj = dff // tn
    vec = pl.BlockSpec((1, d), lambda i, j: (0, 0))
    return pl.pallas_call(
        functools.partial(_ffn_kernel, nj=nj, tpb=tpb, nb=nb, n_ctx=n_ctx, alpha=alpha),
        grid=(m // tm, nj),
        in_specs=[pl.BlockSpec((tm, d), lambda i, j: (i, 0)),
                  pl.BlockSpec((8, d), lambda i, j: (0, 3)),
                  pl.BlockSpec((8, d), lambda i, j: (0, 4)),
                  pl.BlockSpec((8, d), lambda i, j: (0, 5)),
                  vec, vec,
                  pl.BlockSpec((d, tn), lambda i, j: (0, j)),
                  pl.BlockSpec((d, tn), lambda i, j: (0, nj + j)),
                  pl.BlockSpec((tn, d), lambda i, j: (j, 0))],
        out_specs=pl.BlockSpec((tm, d), lambda i, j: (i, 0)),
        out_shape=jax.ShapeDtypeStruct((m, d), F32),
        scratch_shapes=[pltpu.VMEM((tm, d), BF16)],
        compiler_params=_cparams("parallel", "arbitrary"),
        name="ffn",
    )(xs, mod_l, mod_l, mod_l, ln_w.reshape(1, d), ln_b.reshape(1, d), w_in, w_in, w_out)


def _mix_out_kernel(og_ref, os_ref, oa_ref, wg_ref, ws_ref, wa_ref, ga_ref, gb_ref, gc_ref, wo_ref,
                    x_ref, g_ref, lw_ref, lb_ref, o_ref, *, nj, tpb, nb, n_ctx, alpha):
    i = pl.program_id(0)
    j = pl.program_id(1)

    @pl.when(j == 0)
    def _():
        o_ref[...] = jnp.zeros_like(o_ref)

    def gate(ref):
        return jax.nn.sigmoid(ref[...].astype(F32))

    mg = gate(ga_ref) * _dot(og_ref[...], wg_ref[...])
    mg += gate(gb_ref) * _dot(os_ref[...], ws_ref[...])
    mg += gate(gc_ref) * _dot(oa_ref[...], wa_ref[...])
    o_ref[...] += _dot(mg.astype(BF16), wo_ref[...])

    @pl.when(j == nj - 1)
    def _():
        _deepnorm_ln(o_ref, x_ref, g_ref, lw_ref, lb_ref, i, tpb=tpb, nb=nb, n_ctx=n_ctx, alpha=alpha)


def _mix_out(o_gla, o_s5, o_attn, wg, ws, wa, w_out, proj, off_bg, xs, mod_l, ln_w, ln_b, *,
             tm, tpb, nb, n_ctx, alpha):
    m, d = xs.shape
    tn = _pick(d, (256, 128))
    nj = d // tn
    gb0 = off_bg // tn

    def gate_spec(k):
        return pl.BlockSpec((tm, tn), lambda i, j: (i, gb0 + k * nj + j))

    vec = pl.BlockSpec((1, d), lambda i, j: (0, 0))
    return pl.pallas_call(
        functools.partial(_mix_out_kernel, nj=nj, tpb=tpb, nb=nb, n_ctx=n_ctx, alpha=alpha),
        grid=(m // tm, nj),
        in_specs=[pl.BlockSpec((tm, GLA_V), lambda i, j: (i, 0)),
                  pl.BlockSpec((tm, S5_WIDTH), lambda i, j: (i, 0)),
                  pl.BlockSpec((tm, ATTN_Q), lambda i, j: (i, 0)),
                  pl.BlockSpec((GLA_V, tn), lambda i, j: (0, j)),
                  pl.BlockSpec((S5_WIDTH, tn), lambda i, j: (0, j)),
                  pl.BlockSpec((ATTN_Q, tn), lambda i, j: (0, j)),
                  gate_spec(0), gate_spec(1), gate_spec(2),
                  pl.BlockSpec((tn, d), lambda i, j: (j, 0)),
                  pl.BlockSpec((tm, d), lambda i, j: (i, 0)),
                  pl.BlockSpec((8, d), lambda i, j: (0, 2)),
                  vec, vec],
        out_specs=pl.BlockSpec((tm, d), lambda i, j: (i, 0)),
        out_shape=jax.ShapeDtypeStruct((m, d), F32),
        compiler_params=_cparams("parallel", "arbitrary"),
        name="mix_out",
    )(o_gla, o_s5, o_attn, wg, ws, wa, proj, proj, proj, w_out, xs, mod_l, ln_w.reshape(1, d), ln_b.reshape(1, d))


def _log_sigmoid(z):
    return jnp.minimum(z, 0.0) - jnp.log1p(jnp.exp(-jnp.abs(z)))


def _split3(x):
    hi = x.astype(BF16)
    r1 = x - hi.astype(F32)
    mid = r1.astype(BF16)
    lo = (r1 - mid.astype(F32)).astype(BF16)
    return jnp.concatenate([hi, mid, lo], axis=1)


def _sum3(r, w):
    return r[:, 0:w] + r[:, w:2 * w] + r[:, 2 * w:3 * w]


def _gla_kernel(q_ref, k_ref, v_ref, gr_ref, glr_ref, wg_ref, bg_ref, nw_ref, o_ref,
                acc_ref, qe_ref, kd_ref, dec_ref, st_ref, *, n_ctx):
    n_tok = q_ref.shape[0]
    c_len = GLA_CHUNK
    nch = n_tok // c_len
    ncc = n_ctx // c_len
    blk = _pick(n_tok, (256, 128, 64))
    ri = lax.broadcasted_iota(jnp.int32, (blk, blk), 0)
    ci = lax.broadcasted_iota(jnp.int32, (blk, blk), 1)
    same = (ri // c_len) == (ci // c_len)

    for d in range(2):
        keep = jnp.logical_and(same, (ci <= ri) if d == 0 else (ci >= ri))
        sums = jnp.concatenate([jnp.where(keep, 1.0, 0.0), jnp.where(same, 1.0, 0.0)], axis=0).astype(BF16)
        w_gate3 = _split3(wg_ref[d])
        b_gate = bg_ref[d]

        def block(t, carry, d=d, keep=keep, sums=sums, w_gate3=w_gate3, b_gate=b_gate):
            rows = pl.ds(pl.multiple_of(t * blk, blk), blk)
            qc = q_ref[rows, :].astype(F32) * (GLA_DK ** -0.5)
            kc = k_ref[rows, :].astype(F32)
            z = _sum3(_dot(glr_ref[rows, :], w_gate3), GLA_DK) + b_gate
            log_a = _log_sigmoid(z) / GLA_GATE_TAU
            cum = _sum3(_dot(sums, _split3(log_a)), GLA_DK)
            bcum, btot = cum[0:blk], cum[blk:2 * blk]
            qe = (qc * jnp.exp(bcum)).astype(BF16)
            ke = (kc * jnp.exp(-bcum)).astype(BF16)
            att = jnp.where(keep, _dot_nt(qe, ke), 0.0).astype(BF16)
            o = _dot(att, v_ref[rows, :])
            qe_ref[d, rows, :] = qe
            kd_ref[d, rows, :] = (kc * jnp.exp(btot - bcum)).astype(BF16)
            dec_ref[d, rows, :] = jnp.exp(btot)
            if d == 0:
                acc_ref[rows, :] = o
            else:
                acc_ref[rows, :] += o
            return carry

        lax.fori_loop(0, n_tok // blk, block, 0)

    st_ref[...] = jnp.zeros_like(st_ref)

    def chunk(j, carry):
        for d in range(2):
            c = j if d == 0 else jnp.where(j < ncc, ncc - 1 - j, nch - 1 - (j - ncc))
            r0 = pl.multiple_of(c * c_len, c_len)
            rows = pl.ds(r0, c_len)
            st = st_ref[d]
            acc_ref[rows, :] += _dot_nt(qe_ref[d, rows, :], st.astype(BF16))
            st_ref[d] = dec_ref[d, pl.ds(r0, 1), :] * st + _dot_tn(v_ref[rows, :], kd_ref[d, rows, :])
        return carry

    lax.fori_loop(0, nch, chunk, 0)

    nw = nw_ref[...]
    blk = _pick(n_tok, (256, 128, 64))

    def epilogue(t, carry):
        rows = pl.ds(pl.multiple_of(t * blk, blk), blk)
        o = acc_ref[rows, :]
        mu = jnp.mean(o, -1, keepdims=True)
        oc = o - mu
        var = jnp.mean(oc * oc, -1, keepdims=True)
        g = gr_ref[rows, :].astype(F32)
        o_ref[rows, :] = (oc * lax.rsqrt(var + EPS) * nw * (g * jax.nn.sigmoid(g))).astype(BF16)
        return carry

    lax.fori_loop(0, n_tok // blk, epilogue, 0)


def _gla(proj3, offs, w_gate_p, b_gate, norm_w, *, n_ctx):
    nb, n_tok, _ = proj3.shape
    oq, ok, ov, ogr, oglr = (offs[k] for k in ("gq", "gk", "gv", "gr", "glr"))
    return pl.pallas_call(
        functools.partial(_gla_kernel, n_ctx=n_ctx),
        grid=(nb, GLA_HEADS),
        in_specs=[pl.BlockSpec((None, n_tok, GLA_DK), lambda b, h: (b, 0, oq // GLA_DK + h)),
                  pl.BlockSpec((None, n_tok, GLA_DK), lambda b, h: (b, 0, ok // GLA_DK + h)),
                  pl.BlockSpec((None, n_tok, GLA_DV), lambda b, h: (b, 0, ov // GLA_DV + h)),
                  pl.BlockSpec((None, n_tok, GLA_DV), lambda b, h: (b, 0, ogr // GLA_DV + h)),
                  pl.BlockSpec((None, n_tok, LANE), lambda b, h: (b, 0, oglr // LANE)),
                  pl.BlockSpec((2, LANE, GLA_DK), lambda b, h: (0, 0, h)),
                  pl.BlockSpec((2, 1, GLA_DK), lambda b, h: (0, 0, h)),
                  pl.BlockSpec((1, GLA_DV), lambda b, h: (0, h))],
        out_specs=pl.BlockSpec((None, n_tok, GLA_DV), lambda b, h: (b, 0, h)),
        out_shape=jax.ShapeDtypeStruct((nb, n_tok, GLA_V), BF16),
        scratch_shapes=[pltpu.VMEM((n_tok, GLA_DV), F32), pltpu.VMEM((2, n_tok, GLA_DK), BF16),
                        pltpu.VMEM((2, n_tok, GLA_DK), BF16), pltpu.VMEM((2, n_tok, GLA_DK), F32),
                        pltpu.VMEM((2, GLA_DV, GLA_DK), F32)],
        compiler_params=_cparams("parallel", "parallel"),
        name="gla",
    )(proj3, proj3, proj3, proj3, proj3, w_gate_p, b_gate.reshape(2, 1, GLA_QK), norm_w.reshape(1, GLA_V))


def _cmul(x_r, x_i, y_r, y_i):
    return x_r * y_r - x_i * y_i, x_r * y_i + x_i * y_r


def _s5_discretise(lam_r, lam_i, dt):
    mag = jnp.exp(lam_r * dt)
    a_r = mag * jnp.cos(lam_i * dt)
    a_i = mag * jnp.sin(lam_i * dt)
    den = lam_r * lam_r + lam_i * lam_i
    n_r = a_r - 1.0
    k_r = (n_r * lam_r + a_i * lam_i) / den
    k_i = (a_i * lam_r - n_r * lam_i) / den
    return a_r, a_i, k_r, k_i


def _lane_power(a_r, a_i, expo):
    p_r = jnp.ones_like(a_r)
    p_i = jnp.zeros_like(a_i)
    s_r, s_i = a_r, a_i
    for bit in range(4):
        m_r, m_i = _cmul(p_r, p_i, s_r, s_i)
        sel = ((expo >> bit) & 1) == 1
        p_r = jnp.where(sel, m_r, p_r)
        p_i = jnp.where(sel, m_i, p_i)
        if bit < 3:
            s_r, s_i = _cmul(s_r, s_i, s_r, s_i)
    return p_r, p_i


def _s5_param_kernel(lamc_r_ref, lamc_i_ref, lamr_r_ref, lamr_i_ref, ldt_ref, bc_r_ref, bc_i_ref,
                     bt_r_ref, bt_i_ref, ct_r_ref, ct_i_ref, k_ref, min_ref, mout_ref, coef_ref):
    tau = lax.broadcasted_iota(jnp.int32, (S5_STATE, S5_CW), 1) // S5_GROUP
    lane_k = lax.broadcasted_iota(jnp.int32, (S5_GROUP, S5_CW), 1)
    lane_p = lax.broadcasted_iota(jnp.int32, (1, S5_P2), 1)
    rep = (lax.broadcasted_iota(jnp.int32, (S5_GROUP, S5_CW), 1) % S5_GROUP ==
           lax.broadcasted_iota(jnp.int32, (S5_GROUP, S5_CW), 0)).astype(F32)

    for d in range(2):
        dt = jnp.exp(ldt_ref[d])
        a_r, a_i, k_r, k_i = _s5_discretise(lamc_r_ref[d], lamc_i_ref[d], dt)
        bcol_r, bcol_i = _cmul(k_r, k_i, bc_r_ref[...], bc_i_ref[...])
        a_r = jnp.broadcast_to(a_r, (S5_STATE, S5_CW))
        a_i = jnp.broadcast_to(a_i, (S5_STATE, S5_CW))
        up_r, up_i = _lane_power(a_r, a_i, tau)
        dn_r, dn_i = _lane_power(a_r, a_i, S5_L - 1 - tau)
        lag_r, lag_i = (up_r, up_i) if d == 0 else (dn_r, dn_i)
        inp_r, inp_i = (dn_r, dn_i) if d == 0 else (up_r, up_i)
        c_r = _dot_hi(ct_r_ref[d], rep)
        c_i = _dot_hi(ct_i_ref[d], rep)
        wc_r, wc_i = _cmul(lag_r, lag_i, c_r, c_i)
        ar2, ai2, kr2, ki2 = _s5_discretise(lamr_r_ref[d], lamr_i_ref[d], dt)
        btr, bti = _cmul(kr2, ki2, bt_r_ref[...], bt_i_ref[...])
        bstack = jnp.where(lane_p < S5_STATE, btr, -bti)
        g = _dot_hi(bstack, jnp.concatenate([wc_r, wc_i], axis=0))
        for s in range(S5_L):
            if d == 0:
                blk = jnp.where(lane_k >= S5_GROUP * s, pltpu.roll(g, S5_GROUP * s, axis=1), 0.0)
            else:
                shift = (S5_CW - S5_GROUP * (S5_L - 1 - s)) % S5_CW
                blk = jnp.where(lane_k < S5_GROUP * (s + 1), pltpu.roll(g, shift, axis=1), 0.0)
            k_ref[d, S5_GROUP * s:S5_GROUP * (s + 1), :] = blk.astype(BF16)
        mi_r, mi_i = _cmul(inp_r, inp_i, _dot_hi(bcol_r, rep), _dot_hi(bcol_i, rep))
        min_ref[d, 0:S5_STATE, :] = mi_r.astype(BF16)
        min_ref[d, S5_STATE:S5_P2, :] = mi_i.astype(BF16)
        wo_r, wo_i = _cmul(wc_r, wc_i, a_r, a_i)
        mout_ref[d, 0:S5_STATE, :] = wo_r.astype(BF16)
        mout_ref[d, S5_STATE:S5_P2, :] = (-wo_i).astype(BF16)
        pr, pi = ar2, ai2
        for _ in range(4):
            pr, pi = _cmul(pr, pi, pr, pi)
        coef_ref[d, 0:1, :] = pr
        coef_ref[d, 1:2, :] = jnp.where(lane_p < S5_STATE, -pi, pi)
        coef_ref[d, 2:8, :] = jnp.zeros((6, S5_P2), F32)


def _s5_params(lam_re, lam_im, log_dt, b_re, b_im, c_re, c_im):
    depth, _, ng, ns = lam_re.shape
    dup = lambda z: jnp.concatenate([z, z], axis=-1)
    lamc_r = lam_re.reshape(depth, 2, ng, ns, 1)
    lamc_i = lam_im.reshape(depth, 2, ng, ns, 1)
    lamr_r = dup(lam_re).reshape(depth, 2, ng, 1, 2 * ns)
    lamr_i = dup(lam_im).reshape(depth, 2, ng, 1, 2 * ns)
    ldt = log_dt.reshape(depth, 2, ng, 1, 1)
    bt_r = dup(jnp.swapaxes(b_re, -1, -2))
    bt_i = dup(jnp.swapaxes(b_im, -1, -2))
    ct_r = jnp.swapaxes(c_re, -1, -2)
    ct_i = jnp.swapaxes(c_im, -1, -2)

    def per_dir(shape):
        return pl.BlockSpec((None, 2, None) + shape, lambda l, g: (l, 0, g, 0, 0))

    def shared(shape):
        return pl.BlockSpec((None, None) + shape, lambda l, g: (l, g, 0, 0))

    return pl.pallas_call(
        _s5_param_kernel,
        grid=(depth, ng),
        in_specs=[per_dir((ns, 1)), per_dir((ns, 1)), per_dir((1, 2 * ns)), per_dir((1, 2 * ns)),
                  per_dir((1, 1)), shared((ns, S5_GROUP)), shared((ns, S5_GROUP)),
                  shared((S5_GROUP, 2 * ns)), shared((S5_GROUP, 2 * ns)),
                  per_dir((ns, S5_GROUP)), per_dir((ns, S5_GROUP))],
        out_specs=[per_dir((S5_CW, S5_CW)), per_dir((S5_P2, S5_CW)), per_dir((S5_P2, S5_CW)),
                   per_dir((8, S5_P2))],
        out_shape=[jax.ShapeDtypeStruct((depth, 2, ng, S5_CW, S5_CW), BF16),
                   jax.ShapeDtypeStruct((depth, 2, ng, S5_P2, S5_CW), BF16),
                   jax.ShapeDtypeStruct((depth, 2, ng, S5_P2, S5_CW), BF16),
                   jax.ShapeDtypeStruct((depth, 2, ng, 8, S5_P2), F32)],
        compiler_params=_cparams("parallel", "parallel"),
        name="s5_params",
    )(lamc_r, lamc_i, lamr_r, lamr_i, ldt, b_re, b_im, bt_r, bt_i, ct_r, ct_i)


def _s5_kernel(u_ref, k_ref, min_ref, mout_ref, coef_ref, y_ref, s_ref, ssw_ref, x_ref, *, nb):
    rows = u_ref.shape[0]
    rpc = 2 * nb
    u = u_ref[...]
    top = (lax.broadcasted_iota(jnp.int32, (rows, 1), 0) % rpc) < nb
    s = jnp.where(top, _dot_nt(u, min_ref[0]), _dot_nt(u, min_ref[1]))
    s_ref[...] = s
    ssw_ref[...] = pltpu.roll(s, S5_STATE, axis=1)
    top8 = lax.broadcasted_iota(jnp.int32, (rpc, 1), 0) < nb
    c1 = jnp.where(top8, coef_ref[0, 0:1, :], coef_ref[1, 0:1, :])
    c2 = jnp.where(top8, coef_ref[0, 1:2, :], coef_ref[1, 1:2, :])

    def step(j, carry):
        xa, xb = carry
        r = pl.ds(pl.multiple_of(j * rpc, rpc), rpc)
        x_ref[r, :] = xa
        return (c1 * xa + c2 * xb + s_ref[r, :], c1 * xb - c2 * xa + ssw_ref[r, :])

    zero = jnp.zeros((rpc, S5_P2), F32)
    lax.fori_loop(0, rows // rpc, step, (zero, zero))
    xin = x_ref[...].astype(BF16)
    y_f = _dot(u, k_ref[0]) + _dot(xin, mout_ref[0])
    y_b = _dot(u, k_ref[1]) + _dot(xin, mout_ref[1])
    y_ref[...] = jnp.where(top, y_f, y_b)


def _s5_scan(useq, kmat, minm, moutm, coef, layer, *, nb):
    ng, rows, _ = useq.shape

    def per_dir(shape):
        return pl.BlockSpec((None, 2, None) + shape, lambda g: (layer, 0, g, 0, 0))

    return pl.pallas_call(
        functools.partial(_s5_kernel, nb=nb),
        grid=(ng,),
        in_specs=[pl.BlockSpec((None, rows, S5_CW), lambda g: (g, 0, 0)),
                  per_dir((S5_CW, S5_CW)), per_dir((S5_P2, S5_CW)), per_dir((S5_P2, S5_CW)),
                  per_dir((8, S5_P2))],
        out_specs=pl.BlockSpec((None, rows, S5_CW), lambda g: (g, 0, 0)),
        out_shape=jax.ShapeDtypeStruct((ng, rows, S5_CW), F32),
        scratch_shapes=[pltpu.VMEM((rows, S5_P2), F32)] * 3,
        compiler_params=_cparams("parallel"),
        name="s5_scan",
    )(useq, kmat, minm, moutm, coef)


def _s5_out_kernel(yf_ref, yb_ref, su_ref, d_ref, w_ref, o_ref):
    y = yf_ref[...] + yb_ref[...] + su_ref[...].astype(F32) * d_ref[...]
    y = jax.nn.gelu(y)
    o_ref[...] = (y * jax.nn.sigmoid(_dot(y.astype(BF16), w_ref[...]))).astype(BF16)


def _s5_out(y_f, y_b, proj, off_su, d_skip, w_glu, *, tm):
    m = y_f.shape[0]
    row = pl.BlockSpec((tm, S5_WIDTH), lambda i: (i, 0))
    return pl.pallas_call(
        _s5_out_kernel,
        grid=(m // tm,),
        in_specs=[row, row,
                  pl.BlockSpec((tm, S5_WIDTH), lambda i: (i, off_su // S5_WIDTH)),
                  pl.BlockSpec((1, S5_WIDTH), lambda i: (0, 0)),
                  pl.BlockSpec((S5_WIDTH, S5_WIDTH), lambda i: (0, 0))],
        out_specs=row,
        out_shape=jax.ShapeDtypeStruct((m, S5_WIDTH), BF16),
        compiler_params=_cparams("parallel"),
        name="s5_out",
    )(y_f, y_b, proj, d_skip.reshape(1, S5_WIDTH), w_glu)


def _s5_chunk_order(n_ctx, n_tok):
    ncc, nch = n_ctx // S5_L, n_tok // S5_L
    return jnp.concatenate([jnp.arange(ncc - 1, -1, -1), jnp.arange(nch - 1, ncc - 1, -1)])


def _s5_to_chunks(su, n_ctx):
    nb, n_tok, _ = su.shape
    nch = n_tok // S5_L
    u = su.astype(BF16).reshape(nb, nch, S5_L, S5_GROUPS, S5_GROUP)
    u = u.transpose(3, 1, 0, 2, 4).reshape(S5_GROUPS, nch, nb, S5_CW)
    u = jnp.concatenate([u, u[:, _s5_chunk_order(n_ctx, n_tok)]], axis=2)
    return u.reshape(S5_GROUPS, nch * 2 * nb, S5_CW)


def _s5_from_chunks(yseq, nb, n_ctx, n_tok):
    nch = n_tok // S5_L
    y = yseq.reshape(S5_GROUPS, nch, 2 * nb, S5_L, S5_GROUP)
    y_f = y[:, :, :nb]
    y_b = y[:, _s5_chunk_order(n_ctx, n_tok), nb:]

    def back(z):
        return z.transpose(2, 1, 3, 0, 4).reshape(nb * n_tok, S5_WIDTH)

    return back(y_f), back(y_b)


def _rope_tables(n_ctx, n_lat):
    rows = n_lat // GRID_W
    r = jnp.repeat(jnp.arange(rows), GRID_W).astype(F32)
    c = jnp.tile(jnp.arange(GRID_W), rows).astype(F32)
    n_freq = ATTN_HEAD_DIM // 4
    inv = ROPE_THETA ** (-jnp.arange(n_freq, dtype=F32) / n_freq)
    ang = jnp.concatenate([r[:, None] * inv, c[:, None] * inv], -1)
    cos, sin = jnp.cos(ang), jnp.sin(ang)
    cos2 = jnp.repeat(cos, 2, axis=-1)
    sin2 = jnp.stack([-sin, sin], -1).reshape(n_lat, ATTN_HEAD_DIM)
    cos2 = jnp.concatenate([jnp.ones((n_ctx, ATTN_HEAD_DIM), F32), cos2], 0)
    sin2 = jnp.concatenate([jnp.zeros((n_ctx, ATTN_HEAD_DIM), F32), sin2], 0)
    return cos2, sin2


def _qk_prep_kernel(q_ref, k_ref, cos_ref, sin_ref, qw_ref, kw_ref, qo_ref, ko_ref):
    cos = cos_ref[...]
    sin = sin_ref[...]
    even = (lax.broadcasted_iota(jnp.int32, cos.shape, 1) % 2) == 0

    def norm_rope(x, w):
        xn = x * lax.rsqrt(jnp.mean(x * x, -1, keepdims=True) + EPS) * w
        partner = jnp.where(even, pltpu.roll(xn, ATTN_HEAD_DIM - 1, axis=1), pltpu.roll(xn, 1, axis=1))
        return xn * cos + partner * sin

    for h in range(ATTN_Q_HEADS):
        sl = slice(h * ATTN_HEAD_DIM, (h + 1) * ATTN_HEAD_DIM)
        qo_ref[:, sl] = (norm_rope(q_ref[:, sl].astype(F32), qw_ref[...]) * (ATTN_HEAD_DIM ** -0.5)).astype(BF16)
    for h in range(ATTN_KV_HEADS):
        sl = slice(h * ATTN_HEAD_DIM, (h + 1) * ATTN_HEAD_DIM)
        ko_ref[:, sl] = norm_rope(k_ref[:, sl].astype(F32), kw_ref[...]).astype(BF16)


def _qk_prep(proj, offs, cos2, sin2, q_w, k_w, *, tm, tpb):
    m = proj.shape[0]
    oq, ok = offs["aq"], offs["ak"]
    vec = pl.BlockSpec((1, ATTN_HEAD_DIM), lambda i: (0, 0))
    tab = pl.BlockSpec((tm, ATTN_HEAD_DIM), lambda i: (i % tpb, 0))
    return pl.pallas_call(
        _qk_prep_kernel,
        grid=(m // tm,),
        in_specs=[pl.BlockSpec((tm, ATTN_Q), lambda i: (i, oq // ATTN_Q)),
                  pl.BlockSpec((tm, ATTN_KV), lambda i: (i, ok // ATTN_KV)),
                  tab, tab, vec, vec],
        out_specs=[pl.BlockSpec((tm, ATTN_Q), lambda i: (i, 0)),
                   pl.BlockSpec((tm, ATTN_KV), lambda i: (i, 0))],
        out_shape=[jax.ShapeDtypeStruct((m, ATTN_Q), BF16),
                   jax.ShapeDtypeStruct((m, ATTN_KV), BF16)],
        compiler_params=_cparams("parallel"),
        name="qk_prep",
    )(proj, proj, cos2, sin2, q_w.reshape(1, -1), k_w.reshape(1, -1))


def _attn_kernel(q_ref, k_ref, v_ref, o_ref, *, n_ctx, tq):
    def attend(kk, vv):
        for g in range(ATTN_GROUP):
            sl = slice(g * ATTN_HEAD_DIM, (g + 1) * ATTN_HEAD_DIM)
            s = _dot_nt(q_ref[:, sl], kk)
            p = jnp.exp(s - jnp.max(s, -1, keepdims=True))
            l = jnp.sum(p, -1, keepdims=True)
            o_ref[:, sl] = (_dot(p.astype(BF16), vv) / l).astype(BF16)

    is_ctx = pl.program_id(2) < n_ctx // tq

    @pl.when(is_ctx)
    def _():
        attend(k_ref[0:n_ctx, :], v_ref[0:n_ctx, :])

    @pl.when(jnp.logical_not(is_ctx))
    def _():
        attend(k_ref[...], v_ref[...])


def _attention(q3, k3, proj3, off_v, *, n_ctx):
    nb, n_tok, _ = q3.shape
    tq = 256
    qw = ATTN_GROUP * ATTN_HEAD_DIM
    vb0 = off_v // ATTN_HEAD_DIM
    return pl.pallas_call(
        functools.partial(_attn_kernel, n_ctx=n_ctx, tq=tq),
        grid=(nb, ATTN_KV_HEADS, n_tok // tq),
        in_specs=[pl.BlockSpec((None, tq, qw), lambda b, h, i: (b, i, h)),
                  pl.BlockSpec((None, n_tok, ATTN_HEAD_DIM), lambda b, h, i: (b, 0, h)),
                  pl.BlockSpec((None, n_tok, ATTN_HEAD_DIM), lambda b, h, i: (b, 0, vb0 + h))],
        out_specs=pl.BlockSpec((None, tq, qw), lambda b, h, i: (b, i, h)),
        out_shape=jax.ShapeDtypeStruct((nb, n_tok, ATTN_Q), BF16),
        compiler_params=_cparams("parallel", "parallel", "arbitrary"),
        name="attention",
    )(q3, k3, proj3)


def _in_layout(d):
    segs = [("bg", 3 * d), ("gv", GLA_V), ("gr", GLA_V), ("aq", ATTN_Q), ("su", S5_WIDTH),
            ("gq", GLA_QK), ("gk", GLA_QK), ("ak", ATTN_KV), ("av", ATTN_KV), ("glr", GLR_PAD)]
    offs, o = {}, 0
    for name, w in segs:
        offs[name] = o
        o += w
    return segs, offs, o


def _pack_w_in(w_in, d):
    ref_order = [("gq", GLA_QK), ("gk", GLA_QK), ("gv", GLA_V), ("gr", GLA_V), ("glr", GLA_GATE_RANK),
                 ("su", S5_WIDTH), ("aq", ATTN_Q), ("ak", ATTN_KV), ("av", ATTN_KV), ("bg", 3 * d)]
    src, o = {}, 0
    for name, w in ref_order:
        src[name] = (o, w)
        o += w
    segs, _, _ = _in_layout(d)
    parts = []
    for name, w in segs:
        lo, sw = src[name]
        part = w_in[:, :, lo:lo + sw].astype(BF16)
        if sw < w:
            part = jnp.pad(part, ((0, 0), (0, 0), (0, w - sw)))
        parts.append(part)
    return jnp.concatenate(parts, axis=-1)


def kernel(x, c, ctx, c_ctx, w_ada, b_ada, w_in, w_gla_gate, b_gla_gate, gla_norm_w, s5_lam_re, s5_lam_im,
           s5_log_dt, s5_b_re, s5_b_im, s5_c_re, s5_c_im, s5_d, w_s5_glu, q_norm_w, k_norm_w, w_proj_gla,
           w_proj_s5, w_proj_attn, w_out, ln1_w, ln1_b, ln2_w, ln2_b, w_ffn_in, w_ffn_out):
    nb, n_lat, d = x.shape
    n_ctx = ctx.shape[1]
    n_tok = n_ctx + n_lat
    depth = w_in.shape[0]
    alpha = (2 * depth) ** 0.25
    m = nb * n_tok
    tm = _pick(n_tok, (768, 256))
    tpb = n_tok // tm
    assert n_ctx <= tm and n_ctx % 256 == 0 and n_lat % 256 == 0 and nb + 1 <= 8 and (2 * nb) % 8 == 0
    tile = dict(tm=tm, tpb=tpb, nb=nb, n_ctx=n_ctx)
    tm_in = _pick(n_tok, (1152, 768, 256))
    tile_in = dict(tm=tm_in, tpb=n_tok // tm_in, nb=nb, n_ctx=n_ctx)

    _, offs, _ = _in_layout(d)
    for name, width in (("gv", GLA_DV), ("gr", GLA_DV), ("aq", ATTN_GROUP * ATTN_HEAD_DIM), ("su", S5_WIDTH)):
        assert offs[name] % width == 0, (name, offs[name])
    w_in_p = _pack_w_in(w_in, d)
    w_gate_p = jnp.pad(w_gla_gate, ((0, 0), (0, 0), (0, LANE - GLA_GATE_RANK), (0, 0)))
    cos2, sin2 = _rope_tables(n_ctx, n_lat)
    cc = jnp.concatenate([c, c_ctx[None], jnp.zeros((8 - nb - 1, d), F32)], 0)
    mod = _ada_table(cc, w_ada, b_ada)
    kmat, minm, moutm, coef = _s5_params(s5_lam_re, s5_lam_im, s5_log_dt, s5_b_re, s5_b_im, s5_c_re, s5_c_im)

    xs = jnp.concatenate([ctx, x], axis=1).reshape(m, d)
    for l in range(depth):
        mod_l = mod[l]
        proj = _in_proj(xs, mod_l, w_in_p[l], **tile_in)
        proj3 = proj.reshape(nb, n_tok, -1)

        o_gla = _gla(proj3, offs, w_gate_p[l], b_gla_gate[l], gla_norm_w[l], n_ctx=n_ctx).reshape(m, GLA_V)

        su = proj3[:, :, offs["su"]:offs["su"] + S5_WIDTH]
        yseq = _s5_scan(_s5_to_chunks(su, n_ctx), kmat, minm, moutm, coef, l, nb=nb)
        y_f, y_b = _s5_from_chunks(yseq, nb, n_ctx, n_tok)
        o_s5 = _s5_out(y_f, y_b, proj, offs["su"], s5_d[l], w_s5_glu[l].astype(BF16), tm=tm)

        q_r, k_r = _qk_prep(proj, offs, cos2, sin2, q_norm_w[l], k_norm_w[l], tm=tm, tpb=tpb)
        o_attn = _attention(q_r.reshape(nb, n_tok, ATTN_Q), k_r.reshape(nb, n_tok, ATTN_KV),
                            proj3, offs["av"], n_ctx=n_ctx).reshape(m, ATTN_Q)

        xs = _mix_out(o_gla, o_s5, o_attn, w_proj_gla[l].astype(BF16), w_proj_s5[l].astype(BF16),
                      w_proj_attn[l].astype(BF16), w_out[l].astype(BF16), proj, offs["bg"], xs, mod_l,
                      ln1_w[l], ln1_b[l], alpha=alpha, **tile)
        xs = _ffn(xs, mod_l, w_ffn_in[l].astype(BF16), w_ffn_out[l].astype(BF16), ln2_w[l], ln2_b[l],
                  alpha=alpha, **tile)
    return xs.reshape(nb, n_tok, d)[:, n_ctx:]
```

```python
import functools

import jax
import jax.numpy as jnp
from jax import lax
from jax.experimental import pallas as pl
from jax.experimental.pallas import tpu as pltpu

F32 = jnp.float32
BF16 = jnp.bfloat16

GRID_W = 64
GLA_HEADS = 4
GLA_DK = 128
GLA_DV = 256
GLA_QK = GLA_HEADS * GLA_DK
GLA_V = GLA_HEADS * GLA_DV
GLA_GATE_RANK = 16
GLA_GATE_TAU = 16.0
GLA_CHUNK = 64
S5_WIDTH = 768
S5_GROUP = 16
S5_GROUPS = S5_WIDTH // S5_GROUP
S5_STATE = 64
ATTN_Q_HEADS = 8
ATTN_KV_HEADS = 2
ATTN_HEAD_DIM = 128
ATTN_Q = ATTN_Q_HEADS * ATTN_HEAD_DIM
ATTN_KV = ATTN_KV_HEADS * ATTN_HEAD_DIM
ATTN_GROUP = ATTN_Q_HEADS // ATTN_KV_HEADS
ROPE_THETA = 10000.0
EPS = 1e-6

S5_L = 16
S5_CW = S5_L * S5_GROUP
S5_P2 = 2 * S5_STATE
S5_GB = 8
S5_SH = S5_GB * S5_STATE
GLR_PAD = 256

LANE = 128
VMEM_LIMIT = 56 * 1024 * 1024


def _cparams(*sem):
    return pltpu.CompilerParams(dimension_semantics=sem, vmem_limit_bytes=VMEM_LIMIT)


def _pick(n, cands):
    for c in cands:
        if n % c == 0:
            return c
    raise ValueError(f"no tile for {n}")


def _dot(a, b):
    return jnp.dot(a, b, preferred_element_type=F32)


def _dot_nt(a, b):
    return lax.dot_general(a, b, (((1,), (1,)), ((), ())), preferred_element_type=F32)


def _dot_tn(a, b):
    return lax.dot_general(a, b, (((0,), (0,)), ((), ())), preferred_element_type=F32)


def _dot_hi(a, b):
    return jnp.dot(a, b, precision=lax.Precision.HIGHEST, preferred_element_type=F32)


def _head_lat(ref, i, tpb, nb):
    lat = ref[pl.ds(i // tpb, 1), :]
    head = jnp.where(i % tpb == 0, ref[nb:nb + 1, :], lat)
    return head, lat


def _ada_kernel(c_ref, w_ref, b_ref, o_ref):
    cc = c_ref[...]
    s = (cc * jax.nn.sigmoid(cc)).astype(BF16)
    o_ref[...] = _dot(s, w_ref[...].astype(BF16)) + b_ref[...]


def _ada_table(cc, w_ada, b_ada):
    depth, d, d6 = w_ada.shape
    tn = _pick(d6, (1024, 512, 256, 128))
    return pl.pallas_call(
        _ada_kernel,
        grid=(depth, d6 // tn),
        in_specs=[pl.BlockSpec((8, d), lambda l, j: (0, 0)),
                  pl.BlockSpec((None, d, tn), lambda l, j: (l, 0, j)),
                  pl.BlockSpec((None, 1, tn), lambda l, j: (l, 0, j))],
        out_specs=pl.BlockSpec((None, 8, tn), lambda l, j: (l, 0, j)),
        out_shape=jax.ShapeDtypeStruct((depth, 8, d6), F32),
        compiler_params=_cparams("parallel", "parallel"),
        name="ada_table",
    )(cc, w_ada, b_ada.reshape(depth, 1, d6))


def _modulate_to(h_ref, x_ref, sh_ref, sc_ref, i, tpb, nb, n_ctx):
    tm = x_ref.shape[0]
    sh_h, sh_l = _head_lat(sh_ref, i, tpb, nb)
    sc_h, sc_l = _head_lat(sc_ref, i, tpb, nb)
    if n_ctx > 0:
        h_ref[0:n_ctx, :] = (x_ref[0:n_ctx, :] * (1.0 + sc_h) + sh_h).astype(BF16)
    if n_ctx < tm:
        h_ref[n_ctx:tm, :] = (x_ref[n_ctx:tm, :] * (1.0 + sc_l) + sh_l).astype(BF16)


def _inproj_kernel(x_ref, sh_ref, sc_ref, w_ref, o_ref, h_ref, *, tpb, nb, n_ctx):
    i = pl.program_id(0)

    @pl.when(pl.program_id(1) == 0)
    def _():
        _modulate_to(h_ref, x_ref, sh_ref, sc_ref, i, tpb, nb, n_ctx)

    o_ref[...] = _dot(h_ref[...], w_ref[...]).astype(BF16)


def _in_proj(xs, mod_l, w, *, tm, tpb, nb, n_ctx):
    m, d = xs.shape
    wp = w.shape[1]
    tn = _pick(wp, (512, 256, 128))
    return pl.pallas_call(
        functools.partial(_inproj_kernel, tpb=tpb, nb=nb, n_ctx=n_ctx),
        grid=(m // tm, wp // tn),
        in_specs=[pl.BlockSpec((tm, d), lambda i, j: (i, 0)),
                  pl.BlockSpec((8, d), lambda i, j: (0, 0)),
                  pl.BlockSpec((8, d), lambda i, j: (0, 1)),
                  pl.BlockSpec((d, tn), lambda i, j: (0, j))],
        out_specs=pl.BlockSpec((tm, tn), lambda i, j: (i, j)),
        out_shape=jax.ShapeDtypeStruct((m, wp), BF16),
        scratch_shapes=[pltpu.VMEM((tm, d), BF16)],
        compiler_params=_cparams("parallel", "arbitrary"),
        name="in_proj",
    )(xs, mod_l, mod_l, w)


def _deepnorm_ln(o_ref, x_ref, g_ref, lw_ref, lb_ref, i, *, tpb, nb, n_ctx, alpha):
    tm = x_ref.shape[0]
    g_h, g_l = _head_lat(g_ref, i, tpb, nb)

    def seg(lo, hi, g):
        z = alpha * x_ref[lo:hi, :] + g * o_ref[lo:hi, :]
        mu = jnp.mean(z, -1, keepdims=True)
        zc = z - mu
        var = jnp.mean(zc * zc, -1, keepdims=True)
        o_ref[lo:hi, :] = zc * lax.rsqrt(var + EPS) * lw_ref[...] + lb_ref[...]

    if n_ctx > 0:
        seg(0, n_ctx, g_h)
    if n_ctx < tm:
        seg(n_ctx, tm, g_l)


def _ffn_kernel(x_ref, sh_ref, sc_ref, g_ref, lw_ref, lb_ref, wa_ref, wb_ref, wo_ref, o_ref, h_ref, *,
                nj, tpb, nb, n_ctx, alpha):
    i = pl.program_id(0)
    j = pl.program_id(1)

    @pl.when(j == 0)
    def _():
        _modulate_to(h_ref, x_ref, sh_ref, sc_ref, i, tpb, nb, n_ctx)
        o_ref[...] = jnp.zeros_like(o_ref)

    h = h_ref[...]
    a = _dot(h, wa_ref[...])
    b = _dot(h, wb_ref[...])
    act = (a * jax.nn.sigmoid(a) * b).astype(BF16)
    o_ref[...] += _dot(act, wo_ref[...])

    @pl.when(j == nj - 1)
    def _():
        _deepnorm_ln(o_ref, x_ref, g_ref, lw_ref, lb_ref, i, tpb=tpb, nb=nb, n_ctx=n_ctx, alpha=alpha)


def _ffn(xs, mod_l, w_in, w_out, ln_w, ln_b, *, tm, tpb, nb, n_ctx, alpha):
    m, d = xs.shape
    dff = w_out.shape[0]
    tn = _pick(dff, (512, 256, 128))
    nj = dff // tn
    vec = pl.BlockSpec((1, d), lambda i, j: (0, 0))
    return pl.pallas_call(
        functools.partial(_ffn_kernel, nj=nj, tpb=tpb, nb=nb, n_ctx=n_ctx, alpha=alpha),
        grid=(m // tm, nj),
        in_specs=[pl.BlockSpec((tm, d), lambda i, j: (i, 0)),
                  pl.BlockSpec((8, d), lambda i, j: (0, 3)),
                  pl.BlockSpec((8, d), lambda i, j: (0, 4)),
                  pl.BlockSpec((8, d), lambda i, j: (0, 5)),
                  vec, vec,
                  pl.BlockSpec((d, tn), lambda i, j: (0, j)),
                  pl.BlockSpec((d, tn), lambda i, j: (0, nj + j)),
                  pl.BlockSpec((tn, d), lambda i, j: (j, 0))],
        out_specs=pl.BlockSpec((tm, d), lambda i, j: (i, 0)),
        out_shape=jax.ShapeDtypeStruct((m, d), F32),
        scratch_shapes=[pltpu.VMEM((tm, d), BF16)],
        compiler_params=_cparams("parallel", "arbitrary"),
        name="ffn",
    )(xs, mod_l, mod_l, mod_l, ln_w.reshape(1, d), ln_b.reshape(1, d), w_in, w_in, w_out)


def _mix_out_kernel(og_ref, os_ref, oa_ref, wg_ref, ws_ref, wa_ref, ga_ref, gb_ref, gc_ref, wo_ref,
                    x_ref, g_ref, lw_ref, lb_ref, o_ref, *, nj, tpb, nb, n_ctx, alpha):
    i = pl.program_id(0)
    j = pl.program_id(1)

    @pl.when(j == 0)
    def _():
        o_ref[...] = jnp.zeros_like(o_ref)

    def gate(ref):
        return jax.nn.sigmoid(ref[...].astype(F32))

    mg = gate(ga_ref) * _dot(og_ref[...], wg_ref[...])
    mg += gate(gb_ref) * _dot(os_ref[...], ws_ref[...])
    mg += gate(gc_ref) * _dot(oa_ref[...], wa_ref[...])
    o_ref[...] += _dot(mg.astype(BF16), wo_ref[...])

    @pl.when(j == nj - 1)
    def _():
        _deepnorm_ln(o_ref, x_ref, g_ref, lw_ref, lb_ref, i, tpb=tpb, nb=nb, n_ctx=n_ctx, alpha=alpha)


def _mix_out(o_gla, o_s5, o_attn, wg, ws, wa, w_out, proj, off_bg, xs, mod_l, ln_w, ln_b, *,
             tm, tpb, nb, n_ctx, alpha):
    m, d = xs.shape
    tn = _pick(d, (256, 128))
    nj = d // tn
    gb0 = off_bg // tn

    def gate_spec(k):
        return pl.BlockSpec((tm, tn), lambda i, j: (i, gb0 + k * nj + j))

    vec = pl.BlockSpec((1, d), lambda i, j: (0, 0))
    return pl.pallas_call(
        functools.partial(_mix_out_kernel, nj=nj, tpb=tpb, nb=nb, n_ctx=n_ctx, alpha=alpha),
        grid=(m // tm, nj),
        in_specs=[pl.BlockSpec((tm, GLA_V), lambda i, j: (i, 0)),
                  pl.BlockSpec((tm, S5_WIDTH), lambda i, j: (i, 0)),
                  pl.BlockSpec((tm, ATTN_Q), lambda i, j: (i, 0)),
                  pl.BlockSpec((GLA_V, tn), lambda i, j: (0, j)),
                  pl.BlockSpec((S5_WIDTH, tn), lambda i, j: (0, j)),
                  pl.BlockSpec((ATTN_Q, tn), lambda i, j: (0, j)),
                  gate_spec(0), gate_spec(1), gate_spec(2),
                  pl.BlockSpec((tn, d), lambda i, j: (j, 0)),
                  pl.BlockSpec((tm, d), lambda i, j: (i, 0)),
                  pl.BlockSpec((8, d), lambda i, j: (0, 2)),
                  vec, vec],
        out_specs=pl.BlockSpec((tm, d), lambda i, j: (i, 0)),
        out_shape=jax.ShapeDtypeStruct((m, d), F32),
        compiler_params=_cparams("parallel", "arbitrary"),
        name="mix_out",
    )(o_gla, o_s5, o_attn, wg, ws, wa, proj, proj, proj, w_out, xs, mod_l, ln_w.reshape(1, d), ln_b.reshape(1, d))


def _log_sigmoid(z):
    return jnp.minimum(z, 0.0) - jnp.log1p(jnp.exp(-jnp.abs(z)))


def _split3(x):
    hi = x.astype(BF16)
    r1 = x - hi.astype(F32)
    mid = r1.astype(BF16)
    lo = (r1 - mid.astype(F32)).astype(BF16)
    return jnp.concatenate([hi, mid, lo], axis=1)


def _sum3(r, w):
    return r[:, 0:w] + r[:, w:2 * w] + r[:, 2 * w:3 * w]


def _gla_kernel(q_ref, k_ref, v_ref, gr_ref, glr_ref, wg_ref, bg_ref, nw_ref, o_ref,
                acc_ref, qe_ref, kd_ref, dec_ref, st_ref, *, n_ctx):
    n_tok = q_ref.shape[0]
    c_len = GLA_CHUNK
    nch = n_tok // c_len
    ncc = n_ctx // c_len
    blk = _pick(n_tok, (256, 128, 64))
    ri = lax.broadcasted_iota(jnp.int32, (blk, blk), 0)
    ci = lax.broadcasted_iota(jnp.int32, (blk, blk), 1)
    same = (ri // c_len) == (ci // c_len)

    for d in range(2):
        keep = jnp.logical_and(same, (ci <= ri) if d == 0 else (ci >= ri))
        sums = jnp.concatenate([jnp.where(keep, 1.0, 0.0), jnp.where(same, 1.0, 0.0)], axis=0).astype(BF16)
        w_gate3 = _split3(wg_ref[d])
        b_gate = bg_ref[d]

        def block(t, carry, d=d, keep=keep, sums=sums, w_gate3=w_gate3, b_gate=b_gate):
            rows = pl.ds(pl.multiple_of(t * blk, blk), blk)
            qc = q_ref[rows, :].astype(F32) * (GLA_DK ** -0.5)
            kc = k_ref[rows, :].astype(F32)
            z = _sum3(_dot(glr_ref[rows, :], w_gate3), GLA_DK) + b_gate
            log_a = _log_sigmoid(z) / GLA_GATE_TAU
            cum = _sum3(_dot(sums, _split3(log_a)), GLA_DK)
            bcum, btot = cum[0:blk], cum[blk:2 * blk]
            qe = (qc * jnp.exp(bcum)).astype(BF16)
            ke = (kc * jnp.exp(-bcum)).astype(BF16)
            att = jnp.where(keep, _dot_nt(qe, ke), 0.0).astype(BF16)
            o = _dot(att, v_ref[rows, :])
            qe_ref[d, rows, :] = qe
            kd_ref[d, rows, :] = (kc * jnp.exp(btot - bcum)).astype(BF16)
            dec_ref[d, rows, :] = jnp.exp(btot)
            if d == 0:
                acc_ref[rows, :] = o
            else:
                acc_ref[rows, :] += o
            return carry

        lax.fori_loop(0, n_tok // blk, block, 0, unroll=3 if (n_tok // blk) % 3 == 0 else 1)

    st_ref[...] = jnp.zeros_like(st_ref)

    def chunk(j, carry):
        for d in range(2):
            c = j if d == 0 else jnp.where(j < ncc, ncc - 1 - j, nch - 1 - (j - ncc))
            r0 = pl.multiple_of(c * c_len, c_len)
            rows = pl.ds(r0, c_len)
            st = st_ref[d]
            acc_ref[rows, :] += _dot_nt(qe_ref[d, rows, :], st.astype(BF16))
            st_ref[d] = dec_ref[d, pl.ds(r0, 1), :] * st + _dot_tn(v_ref[rows, :], kd_ref[d, rows, :])
        return carry

    lax.fori_loop(0, nch, chunk, 0, unroll=2)

    nw = nw_ref[...]
    blk = _pick(n_tok, (256, 128, 64))

    def epilogue(t, carry):
        rows = pl.ds(pl.multiple_of(t * blk, blk), blk)
        o = acc_ref[rows, :]
        mu = jnp.mean(o, -1, keepdims=True)
        oc = o - mu
        var = jnp.mean(oc * oc, -1, keepdims=True)
        g = gr_ref[rows, :].astype(F32)
        o_ref[rows, :] = (oc * lax.rsqrt(var + EPS) * nw * (g * jax.nn.sigmoid(g))).astype(BF16)
        return carry

    lax.fori_loop(0, n_tok // blk, epilogue, 0)


def _gla(proj3, offs, w_gate_p, b_gate, norm_w, *, n_ctx):
    nb, n_tok, _ = proj3.shape
    oq, ok, ov, ogr, oglr = (offs[k] for k in ("gq", "gk", "gv", "gr", "glr"))
    return pl.pallas_call(
        functools.partial(_gla_kernel, n_ctx=n_ctx),
        grid=(nb, GLA_HEADS),
        in_specs=[pl.BlockSpec((None, n_tok, GLA_DK), lambda b, h: (b, 0, oq // GLA_DK + h)),
                  pl.BlockSpec((None, n_tok, GLA_DK), lambda b, h: (b, 0, ok // GLA_DK + h)),
                  pl.BlockSpec((None, n_tok, GLA_DV), lambda b, h: (b, 0, ov // GLA_DV + h)),
                  pl.BlockSpec((None, n_tok, GLA_DV), lambda b, h: (b, 0, ogr // GLA_DV + h)),
                  pl.BlockSpec((None, n_tok, LANE), lambda b, h: (b, 0, oglr // LANE)),
                  pl.BlockSpec((2, LANE, GLA_DK), lambda b, h: (0, 0, h)),
                  pl.BlockSpec((2, 1, GLA_DK), lambda b, h: (0, 0, h)),
                  pl.BlockSpec((1, GLA_DV), lambda b, h: (0, h))],
        out_specs=pl.BlockSpec((None, n_tok, GLA_DV), lambda b, h: (b, 0, h)),
        out_shape=jax.ShapeDtypeStruct((nb, n_tok, GLA_V), BF16),
        scratch_shapes=[pltpu.VMEM((n_tok, GLA_DV), F32), pltpu.VMEM((2, n_tok, GLA_DK), BF16),
                        pltpu.VMEM((2, n_tok, GLA_DK), BF16), pltpu.VMEM((2, n_tok, GLA_DK), F32),
                        pltpu.VMEM((2, GLA_DV, GLA_DK), F32)],
        compiler_params=_cparams("parallel", "parallel"),
        name="gla",
    )(proj3, proj3, proj3, proj3, proj3, w_gate_p, b_gate.reshape(2, 1, GLA_QK), norm_w.reshape(1, GLA_V))


def _cmul(x_r, x_i, y_r, y_i):
    return x_r * y_r - x_i * y_i, x_r * y_i + x_i * y_r


def _s5_discretise(lam_r, lam_i, dt):
    mag = jnp.exp(lam_r * dt)
    a_r = mag * jnp.cos(lam_i * dt)
    a_i = mag * jnp.sin(lam_i * dt)
    den = lam_r * lam_r + lam_i * lam_i
    n_r = a_r - 1.0
    k_r = (n_r * lam_r + a_i * lam_i) / den
    k_i = (a_i * lam_r - n_r * lam_i) / den
    return a_r, a_i, k_r, k_i


def _lane_power(a_r, a_i, expo):
    p_r = jnp.ones_like(a_r)
    p_i = jnp.zeros_like(a_i)
    s_r, s_i = a_r, a_i
    for bit in range(4):
        m_r, m_i = _cmul(p_r, p_i, s_r, s_i)
        sel = ((expo >> bit) & 1) == 1
        p_r = jnp.where(sel, m_r, p_r)
        p_i = jnp.where(sel, m_i, p_i)
        if bit < 3:
            s_r, s_i = _cmul(s_r, s_i, s_r, s_i)
    return p_r, p_i


def _s5_param_kernel(lamc_r_ref, lamc_i_ref, lamr_r_ref, lamr_i_ref, ldt_ref, bc_r_ref, bc_i_ref,
                     bt_r_ref, bt_i_ref, ct_r_ref, ct_i_ref, tcat_ref, wint_ref, wout_ref, pwr_ref, pwi_ref):
    tau = lax.broadcasted_iota(jnp.int32, (S5_STATE, S5_CW), 1) // S5_GROUP
    lane_p = lax.broadcasted_iota(jnp.int32, (1, S5_P2), 1)
    rep = (lax.broadcasted_iota(jnp.int32, (S5_GROUP, S5_CW), 1) % S5_GROUP ==
           lax.broadcasted_iota(jnp.int32, (S5_GROUP, S5_CW), 0)).astype(F32)
    src = lax.broadcasted_iota(jnp.int32, (S5_CW, S5_L * LANE), 0)
    dst = lax.broadcasted_iota(jnp.int32, (S5_CW, S5_L * LANE), 1)
    same_jc = jnp.logical_and(dst // LANE == src // S5_GROUP, dst % S5_GROUP == src % S5_GROUP)
    dst_g = (dst % LANE) // S5_GROUP

    def group(gl, carry):
        place = jnp.where(jnp.logical_and(same_jc, dst_g == gl), 1.0, 0.0).astype(BF16)

        def spread(z):
            return _dot(z.astype(BF16), place).astype(BF16)

        rows_g = pl.ds(pl.multiple_of(gl * S5_GROUP, S5_GROUP), S5_GROUP)
        rows_re = pl.ds(pl.multiple_of(gl * S5_STATE, S5_STATE), S5_STATE)
        rows_im = pl.ds(pl.multiple_of(S5_SH + gl * S5_STATE, S5_STATE), S5_STATE)
        for d in range(2):
            dt = jnp.exp(ldt_ref[d, gl])
            a_r, a_i, k_r, k_i = _s5_discretise(lamc_r_ref[d, gl], lamc_i_ref[d, gl], dt)
            bcol_r, bcol_i = _cmul(k_r, k_i, bc_r_ref[gl], bc_i_ref[gl])
            a_r = jnp.broadcast_to(a_r, (S5_STATE, S5_CW))
            a_i = jnp.broadcast_to(a_i, (S5_STATE, S5_CW))
            up_r, up_i = _lane_power(a_r, a_i, tau)
            dn_r, dn_i = _lane_power(a_r, a_i, S5_L - 1 - tau)
            lag_r, lag_i = (up_r, up_i) if d == 0 else (dn_r, dn_i)
            inp_r, inp_i = (dn_r, dn_i) if d == 0 else (up_r, up_i)
            c_r = _dot_hi(ct_r_ref[d, gl], rep)
            c_i = _dot_hi(ct_i_ref[d, gl], rep)
            wc_r, wc_i = _cmul(lag_r, lag_i, c_r, c_i)
            ar2, ai2, kr2, ki2 = _s5_discretise(lamr_r_ref[d, gl], lamr_i_ref[d, gl], dt)
            btr, bti = _cmul(kr2, ki2, bt_r_ref[gl], bt_i_ref[gl])
            bstack = jnp.where(lane_p < S5_STATE, btr, -bti)
            g = _dot_hi(bstack, jnp.concatenate([wc_r, wc_i], axis=0))
            mi_r, mi_i = _cmul(inp_r, inp_i, _dot_hi(bcol_r, rep), _dot_hi(bcol_i, rep))
            wo_r, wo_i = _cmul(wc_r, wc_i, a_r, a_i)
            wide = spread(jnp.concatenate([g, mi_r, mi_i, wo_r, -wo_i], axis=0))
            o = S5_GROUP
            tcat_ref[d, rows_g, :] = wide[0:o]
            wint_ref[d, rows_re, :] = wide[o:o + S5_STATE]
            wint_ref[d, rows_im, :] = wide[o + S5_STATE:o + 2 * S5_STATE]
            wout_ref[d, rows_re, :] = wide[o + 2 * S5_STATE:o + 3 * S5_STATE]
            wout_ref[d, rows_im, :] = wide[o + 3 * S5_STATE:o + 4 * S5_STATE]
            pr, pi = ar2, ai2
            for _ in range(4):
                pr, pi = _cmul(pr, pi, pr, pi)
            pw = [(jnp.ones_like(pr), jnp.zeros_like(pi))]
            for _ in range(8):
                pw.append(_cmul(pw[-1][0], pw[-1][1], pr, pi))
            rows = [pw[r] if d == 0 else pw[7 - r] for r in range(8)] + [pw[8], pw[1], pw[2], pw[4]]
            for r, (vr, vi) in enumerate(rows):
                pwr_ref[d, gl, r:r + 1, :] = vr
                pwi_ref[d, gl, r:r + 1, :] = vi
            pwr_ref[d, gl, 12:16, :] = jnp.zeros((4, S5_P2), F32)
            pwi_ref[d, gl, 12:16, :] = jnp.zeros((4, S5_P2), F32)
        return carry

    lax.fori_loop(0, S5_GB, group, 0)


def _s5_params(lam_re, lam_im, log_dt, b_re, b_im, c_re, c_im):
    depth, _, ng, ns = lam_re.shape
    dup = lambda z: jnp.concatenate([z, z], axis=-1)
    lamc_r = lam_re.reshape(depth, 2, ng, ns, 1)
    lamc_i = lam_im.reshape(depth, 2, ng, ns, 1)
    lamr_r = dup(lam_re).reshape(depth, 2, ng, 1, 2 * ns)
    lamr_i = dup(lam_im).reshape(depth, 2, ng, 1, 2 * ns)
    ldt = log_dt.reshape(depth, 2, ng, 1, 1)
    bt_r = dup(jnp.swapaxes(b_re, -1, -2))
    bt_i = dup(jnp.swapaxes(b_im, -1, -2))
    ct_r = jnp.swapaxes(c_re, -1, -2)
    ct_i = jnp.swapaxes(c_im, -1, -2)

    nblk = ng // S5_GB
    wide = S5_L * LANE

    def per_dir(shape):
        return pl.BlockSpec((None, 2, S5_GB) + shape, lambda l, gb: (l, 0, gb, 0, 0))

    def shared(shape):
        return pl.BlockSpec((None, S5_GB) + shape, lambda l, gb: (l, gb, 0, 0))

    def op(shape):
        return pl.BlockSpec((None, 2, None) + shape, lambda l, gb: (l, 0, gb, 0, 0))

    tcat, wint, wout, pwr, pwi = pl.pallas_call(
        _s5_param_kernel,
        grid=(depth, nblk),
        in_specs=[per_dir((ns, 1)), per_dir((ns, 1)), per_dir((1, 2 * ns)), per_dir((1, 2 * ns)),
                  per_dir((1, 1)), shared((ns, S5_GROUP)), shared((ns, S5_GROUP)),
                  shared((S5_GROUP, 2 * ns)), shared((S5_GROUP, 2 * ns)),
                  per_dir((ns, S5_GROUP)), per_dir((ns, S5_GROUP))],
        out_specs=[op((LANE, wide)), op((2 * S5_SH, wide)), op((2 * S5_SH, wide)),
                   per_dir((16, S5_P2)), per_dir((16, S5_P2))],
        out_shape=[jax.ShapeDtypeStruct((depth, 2, nblk, LANE, wide), BF16),
                   jax.ShapeDtypeStruct((depth, 2, nblk, 2 * S5_SH, wide), BF16),
                   jax.ShapeDtypeStruct((depth, 2, nblk, 2 * S5_SH, wide), BF16),
                   jax.ShapeDtypeStruct((depth, 2, ng, 16, S5_P2), F32),
                   jax.ShapeDtypeStruct((depth, 2, ng, 16, S5_P2), F32)],
        compiler_params=_cparams("parallel", "parallel"),
        name="s5_params",
    )(lamc_r, lamc_i, lamr_r, lamr_i, ldt, b_re, b_im, bt_r, bt_i, ct_r, ct_i)

    def table(z):
        z = z[..., :S5_STATE].reshape(depth, 2, nblk, S5_GB, 16, S5_STATE)
        return z.transpose(0, 1, 2, 4, 3, 5).reshape(depth, 2, nblk, 16, S5_SH)

    return tcat, wint, wout, table(pwr), table(pwi)


def _s5_mix_kernel(su_ref, t_ref, wint_ref, wout_ref, pwr_ref, pwi_ref, y_ref,
                   u32_ref, ucat_ref, yall_ref, s_ref, x_ref, *, n_ctx):
    d = pl.program_id(1)
    nb, n_tok, gw = su_ref.shape
    nch = n_tok // S5_L
    sh = S5_SH

    @pl.when(d == 0)
    def _():
        for b in range(nb):
            u32_ref[...] = su_ref[b].astype(F32)
            for s in range(S5_L):
                ucat_ref[b * nch:(b + 1) * nch, s * gw:(s + 1) * gw] = (
                    u32_ref[pl.ds(s, nch, stride=S5_L), :].astype(BF16))
        yall_ref[...] = jnp.zeros_like(yall_ref)

    @pl.when(d == 0)
    def _():
        for s in range(S5_L):
            yall_ref[:, s * gw:] += _dot(ucat_ref[:, s * gw:(s + 1) * gw], t_ref[:, 0:(S5_L - s) * gw])

    @pl.when(d == 1)
    def _():
        for s in range(S5_L):
            yall_ref[:, 0:(s + 1) * gw] += _dot(ucat_ref[:, s * gw:(s + 1) * gw], t_ref[:, (S5_L - 1 - s) * gw:])

    s_ref[...] = _dot_nt(ucat_ref[...], wint_ref[...])
    pr, pi = pwr_ref[...], pwi_ref[...]
    tab_r, tab_i = pr[0:8], pi[0:8]
    a8_r, a8_i = pr[8:9], pi[8:9]
    steps = tuple((k, pr[9 + e:10 + e], pi[9 + e:10 + e]) for e, k in enumerate((1, 2, 4)))
    row8 = lax.broadcasted_iota(jnp.int32, (8, sh), 0)
    ntile = nch // 8
    nct = n_ctx // S5_L // 8

    def scan(fwd):
        def shifted(z, k):
            if fwd:
                return jnp.where(row8 >= k, pltpu.roll(z, k, axis=0), 0.0)
            return jnp.where(row8 < 8 - k, pltpu.roll(z, 8 - k, axis=0), 0.0)

        def body(j, carry):
            out = []
            t = j if fwd else jnp.where(j < nct, nct - 1 - j, ntile - 1 - (j - nct))
            for b in range(nb):
                x_r, x_i = carry[b]
                rows = pl.ds(pl.multiple_of(b * nch + t * 8, 8), 8)
                t_r, t_i = s_ref[rows, 0:sh], s_ref[rows, sh:2 * sh]
                for k, k_r, k_i in steps:
                    s_r, s_i = shifted(t_r, k), shifted(t_i, k)
                    t_r, t_i = t_r + k_r * s_r - k_i * s_i, t_i + k_r * s_i + k_i * s_r
                x_ref[rows, 0:sh] = tab_r * x_r - tab_i * x_i + shifted(t_r, 1)
                x_ref[rows, sh:2 * sh] = tab_r * x_i + tab_i * x_r + shifted(t_i, 1)
                e_r, e_i = (t_r[7:8], t_i[7:8]) if fwd else (t_r[0:1], t_i[0:1])
                out.append((a8_r * x_r - a8_i * x_i + e_r, a8_r * x_i + a8_i * x_r + e_i))
            return tuple(out)

        zero = jnp.zeros((1, sh), F32)
        lax.fori_loop(0, ntile, body, tuple((zero, zero) for _ in range(nb)))

    @pl.when(d == 0)
    def _():
        scan(True)

    @pl.when(d == 1)
    def _():
        scan(False)

    yall_ref[...] += _dot(x_ref[...].astype(BF16), wout_ref[...])

    @pl.when(d == 1)
    def _():
        for b in range(nb):
            for s in range(S5_L):
                y_ref[b, pl.ds(s, nch, stride=S5_L), :] = yall_ref[b * nch:(b + 1) * nch, s * gw:(s + 1) * gw]


def _s5_mix(proj3, off_su, tcat, win, wout, pwr, pwi, layer, *, n_ctx):
    nb, n_tok, _ = proj3.shape
    nblk = tcat.shape[2]
    rows = nb * n_tok // S5_L

    def op(shape):
        return pl.BlockSpec((None, None, None) + shape, lambda gb, d: (layer, d, gb, 0, 0))

    return pl.pallas_call(
        functools.partial(_s5_mix_kernel, n_ctx=n_ctx),
        grid=(nblk, 2),
        in_specs=[pl.BlockSpec((nb, n_tok, LANE), lambda gb, d: (0, 0, off_su // LANE + gb)),
                  op((LANE, S5_L * LANE)), op((2 * S5_SH, S5_L * LANE)), op((2 * S5_SH, S5_L * LANE)),
                  op((16, S5_SH)), op((16, S5_SH))],
        out_specs=pl.BlockSpec((nb, n_tok, LANE), lambda gb, d: (0, 0, gb)),
        out_shape=jax.ShapeDtypeStruct((nb, n_tok, S5_WIDTH), F32),
        scratch_shapes=[pltpu.VMEM((n_tok, LANE), F32), pltpu.VMEM((rows, S5_L * LANE), BF16),
                        pltpu.VMEM((rows, S5_L * LANE), F32), pltpu.VMEM((rows, 2 * S5_SH), F32),
                        pltpu.VMEM((rows, 2 * S5_SH), F32)],
        compiler_params=_cparams("parallel", "arbitrary"),
        name="s5_mix",
    )(proj3, tcat, win, wout, pwr, pwi)


def _s5_out_kernel(y_ref, su_ref, d_ref, w_ref, o_ref):
    y = y_ref[...] + su_ref[...].astype(F32) * d_ref[...]
    y = jax.nn.gelu(y)
    o_ref[...] = (y * jax.nn.sigmoid(_dot(y.astype(BF16), w_ref[...]))).astype(BF16)


def _s5_out(y, proj, off_su, d_skip, w_glu, *, tm):
    m = y.shape[0]
    row = pl.BlockSpec((tm, S5_WIDTH), lambda i: (i, 0))
    return pl.pallas_call(
        _s5_out_kernel,
        grid=(m // tm,),
        in_specs=[row,
                  pl.BlockSpec((tm, S5_WIDTH), lambda i: (i, off_su // S5_WIDTH)),
                  pl.BlockSpec((1, S5_WIDTH), lambda i: (0, 0)),
                  pl.BlockSpec((S5_WIDTH, S5_WIDTH), lambda i: (0, 0))],
        out_specs=row,
        out_shape=jax.ShapeDtypeStruct((m, S5_WIDTH), BF16),
        compiler_params=_cparams("parallel"),
        name="s5_out",
    )(y, proj, d_skip.reshape(1, S5_WIDTH), w_glu)


def _rope_tables(n_ctx, n_lat):
    rows = n_lat // GRID_W
    r = jnp.repeat(jnp.arange(rows), GRID_W).astype(F32)
    c = jnp.tile(jnp.arange(GRID_W), rows).astype(F32)
    n_freq = ATTN_HEAD_DIM // 4
    inv = ROPE_THETA ** (-jnp.arange(n_freq, dtype=F32) / n_freq)
    ang = jnp.concatenate([r[:, None] * inv, c[:, None] * inv], -1)
    cos, sin = jnp.cos(ang), jnp.sin(ang)
    cos2 = jnp.repeat(cos, 2, axis=-1)
    sin2 = jnp.stack([-sin, sin], -1).reshape(n_lat, ATTN_HEAD_DIM)
    cos2 = jnp.concatenate([jnp.ones((n_ctx, ATTN_HEAD_DIM), F32), cos2], 0)
    sin2 = jnp.concatenate([jnp.zeros((n_ctx, ATTN_HEAD_DIM), F32), sin2], 0)
    return cos2, sin2


def _qk_prep_kernel(q_ref, k_ref, cos_ref, sin_ref, qw_ref, kw_ref, qo_ref, ko_ref):
    cos = cos_ref[...]
    sin = sin_ref[...]
    even = (lax.broadcasted_iota(jnp.int32, cos.shape, 1) % 2) == 0

    def norm_rope(x, w):
        xn = x * lax.rsqrt(jnp.mean(x * x, -1, keepdims=True) + EPS) * w
        partner = jnp.where(even, pltpu.roll(xn, ATTN_HEAD_DIM - 1, axis=1), pltpu.roll(xn, 1, axis=1))
        return xn * cos + partner * sin

    for h in range(ATTN_Q_HEADS):
        sl = slice(h * ATTN_HEAD_DIM, (h + 1) * ATTN_HEAD_DIM)
        qo_ref[:, sl] = (norm_rope(q_ref[:, sl].astype(F32), qw_ref[...]) * (ATTN_HEAD_DIM ** -0.5)).astype(BF16)
    for h in range(ATTN_KV_HEADS):
        sl = slice(h * ATTN_HEAD_DIM, (h + 1) * ATTN_HEAD_DIM)
        ko_ref[:, sl] = norm_rope(k_ref[:, sl].astype(F32), kw_ref[...]).astype(BF16)


def _qk_prep(proj, offs, cos2, sin2, q_w, k_w, *, tm, tpb):
    m = proj.shape[0]
    oq, ok = offs["aq"], offs["ak"]
    vec = pl.BlockSpec((1, ATTN_HEAD_DIM), lambda i: (0, 0))
    tab = pl.BlockSpec((tm, ATTN_HEAD_DIM), lambda i: (i % tpb, 0))
    return pl.pallas_call(
        _qk_prep_kernel,
        grid=(m // tm,),
        in_specs=[pl.BlockSpec((tm, ATTN_Q), lambda i: (i, oq // ATTN_Q)),
                  pl.BlockSpec((tm, ATTN_KV), lambda i: (i, ok // ATTN_KV)),
                  tab, tab, vec, vec],
        out_specs=[pl.BlockSpec((tm, ATTN_Q), lambda i: (i, 0)),
                   pl.BlockSpec((tm, ATTN_KV), lambda i: (i, 0))],
        out_shape=[jax.ShapeDtypeStruct((m, ATTN_Q), BF16),
                   jax.ShapeDtypeStruct((m, ATTN_KV), BF16)],
        compiler_params=_cparams("parallel"),
        name="qk_prep",
    )(proj, proj, cos2, sin2, q_w.reshape(1, -1), k_w.reshape(1, -1))


def _attn_kernel(q_ref, k_ref, v_ref, o_ref, *, n_ctx, tq):
    def attend(kk, vv):
        for g in range(ATTN_GROUP):
            sl = slice(g * ATTN_HEAD_DIM, (g + 1) * ATTN_HEAD_DIM)
            s = _dot_nt(q_ref[:, sl], kk)
            p = jnp.exp(s - jnp.max(s, -1, keepdims=True))
            l = jnp.sum(p, -1, keepdims=True)
            o_ref[:, sl] = (_dot(p.astype(BF16), vv) / l).astype(BF16)

    is_ctx = pl.program_id(2) < n_ctx // tq

    @pl.when(is_ctx)
    def _():
        attend(k_ref[0:n_ctx, :], v_ref[0:n_ctx, :])

    @pl.when(jnp.logical_not(is_ctx))
    def _():
        attend(k_ref[...], v_ref[...])


def _attention(q3, k3, proj3, off_v, *, n_ctx):
    nb, n_tok, _ = q3.shape
    tq = 256
    qw = ATTN_GROUP * ATTN_HEAD_DIM
    vb0 = off_v // ATTN_HEAD_DIM
    return pl.pallas_call(
        functools.partial(_attn_kernel, n_ctx=n_ctx, tq=tq),
        grid=(nb, ATTN_KV_HEADS, n_tok // tq),
        in_specs=[pl.BlockSpec((None, tq, qw), lambda b, h, i: (b, i, h)),
                  pl.BlockSpec((None, n_tok, ATTN_HEAD_DIM), lambda b, h, i: (b, 0, h)),
                  pl.BlockSpec((None, n_tok, ATTN_HEAD_DIM), lambda b, h, i: (b, 0, vb0 + h))],
        out_specs=pl.BlockSpec((None, tq, qw), lambda b, h, i: (b, i, h)),
        out_shape=jax.ShapeDtypeStruct((nb, n_tok, ATTN_Q), BF16),
        compiler_params=_cparams("parallel", "parallel", "arbitrary"),
        name="attention",
    )(q3, k3, proj3)


def _in_layout(d):
    segs = [("bg", 3 * d), ("gv", GLA_V), ("gr", GLA_V), ("aq", ATTN_Q), ("su", S5_WIDTH),
            ("gq", GLA_QK), ("gk", GLA_QK), ("ak", ATTN_KV), ("av", ATTN_KV), ("glr", GLR_PAD)]
    offs, o = {}, 0
    for name, w in segs:
        offs[name] = o
        o += w
    return segs, offs, o


def _pack_w_in(w_in, d):
    ref_order = [("gq", GLA_QK), ("gk", GLA_QK), ("gv", GLA_V), ("gr", GLA_V), ("glr", GLA_GATE_RANK),
                 ("su", S5_WIDTH), ("aq", ATTN_Q), ("ak", ATTN_KV), ("av", ATTN_KV), ("bg", 3 * d)]
    src, o = {}, 0
    for name, w in ref_order:
        src[name] = (o, w)
        o += w
    segs, _, _ = _in_layout(d)
    parts = []
    for name, w in segs:
        lo, sw = src[name]
        part = w_in[:, :, lo:lo + sw].astype(BF16)
        if sw < w:
            part = jnp.pad(part, ((0, 0), (0, 0), (0, w - sw)))
        parts.append(part)
    return jnp.concatenate(parts, axis=-1)


def kernel(x, c, ctx, c_ctx, w_ada, b_ada, w_in, w_gla_gate, b_gla_gate, gla_norm_w, s5_lam_re, s5_lam_im,
           s5_log_dt, s5_b_re, s5_b_im, s5_c_re, s5_c_im, s5_d, w_s5_glu, q_norm_w, k_norm_w, w_proj_gla,
           w_proj_s5, w_proj_attn, w_out, ln1_w, ln1_b, ln2_w, ln2_b, w_ffn_in, w_ffn_out):
    nb, n_lat, d = x.shape
    n_ctx = ctx.shape[1]
    n_tok = n_ctx + n_lat
    depth = w_in.shape[0]
    alpha = (2 * depth) ** 0.25
    m = nb * n_tok
    tm = _pick(n_tok, (768, 256))
    tpb = n_tok // tm
    assert n_ctx <= tm and n_ctx % 256 == 0 and n_lat % 256 == 0 and nb + 1 <= 8
    assert (n_ctx // S5_L) % 8 == 0 and (n_tok // S5_L) % 8 == 0 and S5_GB * S5_GROUP == LANE
    tile = dict(tm=tm, tpb=tpb, nb=nb, n_ctx=n_ctx)
    tm_in = _pick(n_tok, (1152, 768, 256))
    tile_in = dict(tm=tm_in, tpb=n_tok // tm_in, nb=nb, n_ctx=n_ctx)

    _, offs, _ = _in_layout(d)
    for name, width in (("gv", GLA_DV), ("gr", GLA_DV), ("aq", ATTN_GROUP * ATTN_HEAD_DIM), ("su", S5_WIDTH)):
        assert offs[name] % width == 0, (name, offs[name])
    w_in_p = _pack_w_in(w_in, d)
    w_gate_p = jnp.pad(w_gla_gate, ((0, 0), (0, 0), (0, LANE - GLA_GATE_RANK), (0, 0)))
    cos2, sin2 = _rope_tables(n_ctx, n_lat)
    cc = jnp.concatenate([c, c_ctx[None], jnp.zeros((8 - nb - 1, d), F32)], 0)
    mod = _ada_table(cc, w_ada, b_ada)
    s5_ops = _s5_params(s5_lam_re, s5_lam_im, s5_log_dt, s5_b_re, s5_b_im, s5_c_re, s5_c_im)

    xs = jnp.concatenate([ctx, x], axis=1).reshape(m, d)
    for l in range(depth):
        mod_l = mod[l]
        proj = _in_proj(xs, mod_l, w_in_p[l], **tile_in)
        proj3 = proj.reshape(nb, n_tok, -1)

        o_gla = _gla(proj3, offs, w_gate_p[l], b_gla_gate[l], gla_norm_w[l], n_ctx=n_ctx).reshape(m, GLA_V)

        y_s5 = _s5_mix(proj3, offs["su"], *s5_ops, l, n_ctx=n_ctx).reshape(m, S5_WIDTH)
        o_s5 = _s5_out(y_s5, proj, offs["su"], s5_d[l], w_s5_glu[l].astype(BF16), tm=tm)

        q_r, k_r = _qk_prep(proj, offs, cos2, sin2, q_norm_w[l], k_norm_w[l], tm=tm, tpb=tpb)
        o_attn = _attention(q_r.reshape(nb, n_tok, ATTN_Q), k_r.reshape(nb, n_tok, ATTN_KV),
                            proj3, offs["av"], n_ctx=n_ctx).reshape(m, ATTN_Q)

        xs = _mix_out(o_gla, o_s5, o_attn, w_proj_gla[l].astype(BF16), w_proj_s5[l].astype(BF16),
                      w_proj_attn[l].astype(BF16), w_out[l].astype(BF16), proj, offs["bg"], xs, mod_l,
                      ln1_w[l], ln1_b[l], alpha=alpha, **tile)
        xs = _ffn(xs, mod_l, w_ffn_in[l].astype(BF16), w_ffn_out[l].astype(BF16), ln2_w[l], ln2_b[l],
                  alpha=alpha, **tile)
    return xs.reshape(nb, n_tok, d)[:, n_ctx:]
```

```python
import functools

import jax
import jax.numpy as jnp
from jax import lax
from jax.experimental import pallas as pl
from jax.experimental.pallas import tpu as pltpu

F32 = jnp.float32
BF16 = jnp.bfloat16

GRID_W = 64
GLA_HEADS = 4
GLA_DK = 128
GLA_DV = 256
GLA_QK = GLA_HEADS * GLA_DK
GLA_V = GLA_HEADS * GLA_DV
GLA_GATE_RANK = 16
GLA_GATE_TAU = 16.0
GLA_CHUNK = 64
S5_WIDTH = 768
S5_GROUP = 16
S5_GROUPS = S5_WIDTH // S5_GROUP
S5_STATE = 64
ATTN_Q_HEADS = 8
ATTN_KV_HEADS = 2
ATTN_HEAD_DIM = 128
ATTN_Q = ATTN_Q_HEADS * ATTN_HEAD_DIM
ATTN_KV = ATTN_KV_HEADS * ATTN_HEAD_DIM
ATTN_GROUP = ATTN_Q_HEADS // ATTN_KV_HEADS
ROPE_THETA = 10000.0
EPS = 1e-6

S5_L = 16
S5_CW = S5_L * S5_GROUP
S5_P2 = 2 * S5_STATE
S5_GB = 8
S5_SH = S5_GB * S5_STATE
GLR_PAD = 256

LANE = 128
VMEM_LIMIT = 56 * 1024 * 1024


def _cparams(*sem):
    return pltpu.CompilerParams(dimension_semantics=sem, vmem_limit_bytes=VMEM_LIMIT)


def _pick(n, cands):
    for c in cands:
        if n % c == 0:
            return c
    raise ValueError(f"no tile for {n}")


def _dot(a, b):
    return jnp.dot(a, b, preferred_element_type=F32)


def _dot_nt(a, b):
    return lax.dot_general(a, b, (((1,), (1,)), ((), ())), preferred_element_type=F32)


def _dot_tn(a, b):
    return lax.dot_general(a, b, (((0,), (0,)), ((), ())), preferred_element_type=F32)


def _dot_hi(a, b):
    return jnp.dot(a, b, precision=lax.Precision.HIGHEST, preferred_element_type=F32)


def _head_lat(ref, i, tpb, nb):
    lat = ref[pl.ds(i // tpb, 1), :]
    head = jnp.where(i % tpb == 0, ref[nb:nb + 1, :], lat)
    return head, lat


def _ada_kernel(c_ref, w_ref, b_ref, o_ref):
    cc = c_ref[...]
    s = (cc * jax.nn.sigmoid(cc)).astype(BF16)
    o_ref[...] = _dot(s, w_ref[...].astype(BF16)) + b_ref[...]


def _ada_table(cc, w_ada, b_ada):
    depth, d, d6 = w_ada.shape
    tn = _pick(d6, (1024, 512, 256, 128))
    return pl.pallas_call(
        _ada_kernel,
        grid=(depth, d6 // tn),
        in_specs=[pl.BlockSpec((8, d), lambda l, j: (0, 0)),
                  pl.BlockSpec((None, d, tn), lambda l, j: (l, 0, j)),
                  pl.BlockSpec((None, 1, tn), lambda l, j: (l, 0, j))],
        out_specs=pl.BlockSpec((None, 8, tn), lambda l, j: (l, 0, j)),
        out_shape=jax.ShapeDtypeStruct((depth, 8, d6), F32),
        compiler_params=_cparams("parallel", "parallel"),
        name="ada_table",
    )(cc, w_ada, b_ada.reshape(depth, 1, d6))


def _modulate_to(h_ref, x_ref, sh_ref, sc_ref, i, tpb, nb, n_ctx):
    tm = x_ref.shape[0]
    sh_h, sh_l = _head_lat(sh_ref, i, tpb, nb)
    sc_h, sc_l = _head_lat(sc_ref, i, tpb, nb)
    if n_ctx > 0:
        h_ref[0:n_ctx, :] = (x_ref[0:n_ctx, :] * (1.0 + sc_h) + sh_h).astype(BF16)
    if n_ctx < tm:
        h_ref[n_ctx:tm, :] = (x_ref[n_ctx:tm, :] * (1.0 + sc_l) + sh_l).astype(BF16)


def _inproj_kernel(x_ref, sh_ref, sc_ref, w_ref, o_ref, h_ref, *, tpb, nb, n_ctx):
    i = pl.program_id(0)

    @pl.when(pl.program_id(1) == 0)
    def _():
        _modulate_to(h_ref, x_ref, sh_ref, sc_ref, i, tpb, nb, n_ctx)

    o_ref[...] = _dot(h_ref[...], w_ref[...]).astype(BF16)


def _in_proj(xs, mod_l, w, *, tm, tpb, nb, n_ctx):
    m, d = xs.shape
    wp = w.shape[1]
    tn = _pick(wp, (512, 256, 128))
    return pl.pallas_call(
        functools.partial(_inproj_kernel, tpb=tpb, nb=nb, n_ctx=n_ctx),
        grid=(m // tm, wp // tn),
        in_specs=[pl.BlockSpec((tm, d), lambda i, j: (i, 0)),
                  pl.BlockSpec((8, d), lambda i, j: (0, 0)),
                  pl.BlockSpec((8, d), lambda i, j: (0, 1)),
                  pl.BlockSpec((d, tn), lambda i, j: (0, j))],
        out_specs=pl.BlockSpec((tm, tn), lambda i, j: (i, j)),
        out_shape=jax.ShapeDtypeStruct((m, wp), BF16),
        scratch_shapes=[pltpu.VMEM((tm, d), BF16)],
        compiler_params=_cparams("parallel", "arbitrary"),
        name="in_proj",
    )(xs, mod_l, mod_l, w)


def _deepnorm_ln(o_ref, x_ref, g_ref, lw_ref, lb_ref, i, *, tpb, nb, n_ctx, alpha):
    tm = x_ref.shape[0]
    g_h, g_l = _head_lat(g_ref, i, tpb, nb)

    def seg(lo, hi, g):
        z = alpha * x_ref[lo:hi, :] + g * o_ref[lo:hi, :]
        mu = jnp.mean(z, -1, keepdims=True)
        zc = z - mu
        var = jnp.mean(zc * zc, -1, keepdims=True)
        o_ref[lo:hi, :] = zc * lax.rsqrt(var + EPS) * lw_ref[...] + lb_ref[...]

    if n_ctx > 0:
        seg(0, n_ctx, g_h)
    if n_ctx < tm:
        seg(n_ctx, tm, g_l)


def _ffn_kernel(x_ref, sh_ref, sc_ref, g_ref, lw_ref, lb_ref, wa_ref, wb_ref, wo_ref, o_ref, h_ref, *,
                nj, tpb, nb, n_ctx, alpha):
    i = pl.program_id(0)
    j = pl.program_id(1)

    @pl.when(j == 0)
    def _():
        _modulate_to(h_ref, x_ref, sh_ref, sc_ref, i, tpb, nb, n_ctx)
        o_ref[...] = jnp.zeros_like(o_ref)

    h = h_ref[...]
    a = _dot(h, wa_ref[...])
    b = _dot(h, wb_ref[...])
    act = (a * jax.nn.sigmoid(a) * b).astype(BF16)
    o_ref[...] += _dot(act, wo_ref[...])

    @pl.when(j == nj - 1)
    def _():
        _deepnorm_ln(o_ref, x_ref, g_ref, lw_ref, lb_ref, i, tpb=tpb, nb=nb, n_ctx=n_ctx, alpha=alpha)


def _ffn(xs, mod_l, w_in, w_out, ln_w, ln_b, *, tm, tpb, nb, n_ctx, alpha):
    m, d = xs.shape
    dff = w_out.shape[0]
    tn = _pick(dff, (512, 256, 128))
    nj = dff // tn
    vec = pl.BlockSpec((1, d), lambda i, j: (0, 0))
    return pl.pallas_call(
        functools.partial(_ffn_kernel, nj=nj, tpb=tpb, nb=nb, n_ctx=n_ctx, alpha=alpha),
        grid=(m // tm, nj),
        in_specs=[pl.BlockSpec((tm, d), lambda i, j: (i, 0)),
                  pl.BlockSpec((8, d), lambda i, j: (0, 3)),
                  pl.BlockSpec((8, d), lambda i, j: (0, 4)),
                  pl.BlockSpec((8, d), lambda i, j: (0, 5)),
                  vec, vec,
                  pl.BlockSpec((d, tn), lambda i, j: (0, j)),
                  pl.BlockSpec((d, tn), lambda i, j: (0, nj + j)),
                  pl.BlockSpec((tn, d), lambda i, j: (j, 0))],
        out_specs=pl.BlockSpec((tm, d), lambda i, j: (i, 0)),
        out_shape=jax.ShapeDtypeStruct((m, d), F32),
        scratch_shapes=[pltpu.VMEM((tm, d), BF16)],
        compiler_params=_cparams("parallel", "arbitrary"),
        name="ffn",
    )(xs, mod_l, mod_l, mod_l, ln_w.reshape(1, d), ln_b.reshape(1, d), w_in, w_in, w_out)


def _mix_out_kernel(og_ref, os_ref, oa_ref, wg_ref, ws_ref, wa_ref, ga_ref, gb_ref, gc_ref, wo_ref,
                    x_ref, g_ref, lw_ref, lb_ref, o_ref, *, nj, tpb, nb, n_ctx, alpha):
    i = pl.program_id(0)
    j = pl.program_id(1)

    @pl.when(j == 0)
    def _():
        o_ref[...] = jnp.zeros_like(o_ref)

    def gate(ref):
        return jax.nn.sigmoid(ref[...].astype(F32))

    mg = gate(ga_ref) * _dot(og_ref[...], wg_ref[...])
    mg += gate(gb_ref) * _dot(os_ref[...], ws_ref[...])
    mg += gate(gc_ref) * _dot(oa_ref[...], wa_ref[...])
    o_ref[...] += _dot(mg.astype(BF16), wo_ref[...])

    @pl.when(j == nj - 1)
    def _():
        _deepnorm_ln(o_ref, x_ref, g_ref, lw_ref, lb_ref, i, tpb=tpb, nb=nb, n_ctx=n_ctx, alpha=alpha)


def _mix_out(o_gla, o_s5, o_attn, wg, ws, wa, w_out, proj, off_bg, xs, mod_l, ln_w, ln_b, *,
             tm, tpb, nb, n_ctx, alpha):
    m, d = xs.shape
    tn = _pick(d, (256, 128))
    nj = d // tn
    gb0 = off_bg // tn

    def gate_spec(k):
        return pl.BlockSpec((tm, tn), lambda i, j: (i, gb0 + k * nj + j))

    vec = pl.BlockSpec((1, d), lambda i, j: (0, 0))
    return pl.pallas_call(
        functools.partial(_mix_out_kernel, nj=nj, tpb=tpb, nb=nb, n_ctx=n_ctx, alpha=alpha),
        grid=(m // tm, nj),
        in_specs=[pl.BlockSpec((tm, GLA_V), lambda i, j: (i, 0)),
                  pl.BlockSpec((tm, S5_WIDTH), lambda i, j: (i, 0)),
                  pl.BlockSpec((tm, ATTN_Q), lambda i, j: (i, 0)),
                  pl.BlockSpec((GLA_V, tn), lambda i, j: (0, j)),
                  pl.BlockSpec((S5_WIDTH, tn), lambda i, j: (0, j)),
                  pl.BlockSpec((ATTN_Q, tn), lambda i, j: (0, j)),
                  gate_spec(0), gate_spec(1), gate_spec(2),
                  pl.BlockSpec((tn, d), lambda i, j: (j, 0)),
                  pl.BlockSpec((tm, d), lambda i, j: (i, 0)),
                  pl.BlockSpec((8, d), lambda i, j: (0, 2)),
                  vec, vec],
        out_specs=pl.BlockSpec((tm, d), lambda i, j: (i, 0)),
        out_shape=jax.ShapeDtypeStruct((m, d), F32),
        compiler_params=_cparams("parallel", "arbitrary"),
        name="mix_out",
    )(o_gla, o_s5, o_attn, wg, ws, wa, proj, proj, proj, w_out, xs, mod_l, ln_w.reshape(1, d), ln_b.reshape(1, d))


def _log_sigmoid(z):
    return jnp.minimum(z, 0.0) - jnp.log1p(jnp.exp(-jnp.abs(z)))


def _split3(x):
    hi = x.astype(BF16)
    r1 = x - hi.astype(F32)
    mid = r1.astype(BF16)
    lo = (r1 - mid.astype(F32)).astype(BF16)
    return jnp.concatenate([hi, mid, lo], axis=1)


def _sum3(r, w):
    return r[:, 0:w] + r[:, w:2 * w] + r[:, 2 * w:3 * w]


def _gla_kernel(q_ref, k_ref, v_ref, gr_ref, glr_ref, wg_ref, bg_ref, nw_ref, o_ref,
                acc_ref, qe_ref, kd_ref, dec_ref, st_ref, *, n_ctx):
    n_tok = q_ref.shape[0]
    c_len = GLA_CHUNK
    nch = n_tok // c_len
    ncc = n_ctx // c_len
    blk = _pick(n_tok, (256, 128, 64))
    ri = lax.broadcasted_iota(jnp.int32, (blk, blk), 0)
    ci = lax.broadcasted_iota(jnp.int32, (blk, blk), 1)
    same = (ri // c_len) == (ci // c_len)

    for d in range(2):
        keep = jnp.logical_and(same, (ci <= ri) if d == 0 else (ci >= ri))
        sums = jnp.concatenate([jnp.where(keep, 1.0, 0.0), jnp.where(same, 1.0, 0.0)], axis=0).astype(BF16)
        w_gate3 = _split3(wg_ref[d])
        b_gate = bg_ref[d]

        def block(t, carry, d=d, keep=keep, sums=sums, w_gate3=w_gate3, b_gate=b_gate):
            rows = pl.ds(pl.multiple_of(t * blk, blk), blk)
            qc = q_ref[rows, :].astype(F32) * (GLA_DK ** -0.5)
            kc = k_ref[rows, :].astype(F32)
            z = _sum3(_dot(glr_ref[rows, :], w_gate3), GLA_DK) + b_gate
            log_a = _log_sigmoid(z) / GLA_GATE_TAU
            cum = _sum3(_dot(sums, _split3(log_a)), GLA_DK)
            bcum, btot = cum[0:blk], cum[blk:2 * blk]
            qe = (qc * jnp.exp(bcum)).astype(BF16)
            ke = (kc * jnp.exp(-bcum)).astype(BF16)
            att = jnp.where(keep, _dot_nt(qe, ke), 0.0).astype(BF16)
            o = _dot(att, v_ref[rows, :])
            qe_ref[d, rows, :] = qe
            kd_ref[d, rows, :] = (kc * jnp.exp(btot - bcum)).astype(BF16)
            dec_ref[d, rows, :] = jnp.exp(btot)
            if d == 0:
                acc_ref[rows, :] = o
            else:
                acc_ref[rows, :] += o
            return carry

        lax.fori_loop(0, n_tok // blk, block, 0, unroll=3 if (n_tok // blk) % 3 == 0 else 1)

    st_ref[...] = jnp.zeros_like(st_ref)

    def chunk(j, carry):
        for d in range(2):
            c = j if d == 0 else jnp.where(j < ncc, ncc - 1 - j, nch - 1 - (j - ncc))
            r0 = pl.multiple_of(c * c_len, c_len)
            rows = pl.ds(r0, c_len)
            st = st_ref[d]
            acc_ref[rows, :] += _dot_nt(qe_ref[d, rows, :], st.astype(BF16))
            st_ref[d] = dec_ref[d, pl.ds(r0, 1), :] * st + _dot_tn(v_ref[rows, :], kd_ref[d, rows, :])
        return carry

    lax.fori_loop(0, nch, chunk, 0, unroll=2)

    nw = nw_ref[...]
    blk = _pick(n_tok, (256, 128, 64))

    def epilogue(t, carry):
        rows = pl.ds(pl.multiple_of(t * blk, blk), blk)
        o = acc_ref[rows, :]
        mu = jnp.mean(o, -1, keepdims=True)
        oc = o - mu
        var = jnp.mean(oc * oc, -1, keepdims=True)
        g = gr_ref[rows, :].astype(F32)
        o_ref[rows, :] = (oc * lax.rsqrt(var + EPS) * nw * (g * jax.nn.sigmoid(g))).astype(BF16)
        return carry

    lax.fori_loop(0, n_tok // blk, epilogue, 0)


def _gla(proj3, offs, w_gate_p, b_gate, norm_w, *, n_ctx):
    nb, n_tok, _ = proj3.shape
    oq, ok, ov, ogr, oglr = (offs[k] for k in ("gq", "gk", "gv", "gr", "glr"))
    return pl.pallas_call(
        functools.partial(_gla_kernel, n_ctx=n_ctx),
        grid=(nb, GLA_HEADS),
        in_specs=[pl.BlockSpec((None, n_tok, GLA_DK), lambda b, h: (b, 0, oq // GLA_DK + h)),
                  pl.BlockSpec((None, n_tok, GLA_DK), lambda b, h: (b, 0, ok // GLA_DK + h)),
                  pl.BlockSpec((None, n_tok, GLA_DV), lambda b, h: (b, 0, ov // GLA_DV + h)),
                  pl.BlockSpec((None, n_tok, GLA_DV), lambda b, h: (b, 0, ogr // GLA_DV + h)),
                  pl.BlockSpec((None, n_tok, LANE), lambda b, h: (b, 0, oglr // LANE)),
                  pl.BlockSpec((2, LANE, GLA_DK), lambda b, h: (0, 0, h)),
                  pl.BlockSpec((2, 1, GLA_DK), lambda b, h: (0, 0, h)),
                  pl.BlockSpec((1, GLA_DV), lambda b, h: (0, h))],
        out_specs=pl.BlockSpec((None, n_tok, GLA_DV), lambda b, h: (b, 0, h)),
        out_shape=jax.ShapeDtypeStruct((nb, n_tok, GLA_V), BF16),
        scratch_shapes=[pltpu.VMEM((n_tok, GLA_DV), F32), pltpu.VMEM((2, n_tok, GLA_DK), BF16),
                        pltpu.VMEM((2, n_tok, GLA_DK), BF16), pltpu.VMEM((2, n_tok, GLA_DK), F32),
                        pltpu.VMEM((2, GLA_DV, GLA_DK), F32)],
        compiler_params=_cparams("parallel", "parallel"),
        name="gla",
    )(proj3, proj3, proj3, proj3, proj3, w_gate_p, b_gate.reshape(2, 1, GLA_QK), norm_w.reshape(1, GLA_V))


def _cmul(x_r, x_i, y_r, y_i):
    return x_r * y_r - x_i * y_i, x_r * y_i + x_i * y_r


def _s5_discretise(lam_r, lam_i, dt):
    mag = jnp.exp(lam_r * dt)
    a_r = mag * jnp.cos(lam_i * dt)
    a_i = mag * jnp.sin(lam_i * dt)
    den = lam_r * lam_r + lam_i * lam_i
    n_r = a_r - 1.0
    k_r = (n_r * lam_r + a_i * lam_i) / den
    k_i = (a_i * lam_r - n_r * lam_i) / den
    return a_r, a_i, k_r, k_i


def _lane_power(a_r, a_i, expo):
    p_r = jnp.ones_like(a_r)
    p_i = jnp.zeros_like(a_i)
    s_r, s_i = a_r, a_i
    for bit in range(4):
        m_r, m_i = _cmul(p_r, p_i, s_r, s_i)
        sel = ((expo >> bit) & 1) == 1
        p_r = jnp.where(sel, m_r, p_r)
        p_i = jnp.where(sel, m_i, p_i)
        if bit < 3:
            s_r, s_i = _cmul(s_r, s_i, s_r, s_i)
    return p_r, p_i


def _s5_param_kernel(lamc_r_ref, lamc_i_ref, lamr_r_ref, lamr_i_ref, ldt_ref, bc_r_ref, bc_i_ref,
                     bt_r_ref, bt_i_ref, ct_r_ref, ct_i_ref, tcat_ref, wint_ref, wout_ref, pwr_ref, pwi_ref):
    tau = lax.broadcasted_iota(jnp.int32, (S5_STATE, S5_CW), 1) // S5_GROUP
    lane_p = lax.broadcasted_iota(jnp.int32, (1, S5_P2), 1)
    rep = (lax.broadcasted_iota(jnp.int32, (S5_GROUP, S5_CW), 1) % S5_GROUP ==
           lax.broadcasted_iota(jnp.int32, (S5_GROUP, S5_CW), 0)).astype(F32)
    src = lax.broadcasted_iota(jnp.int32, (S5_CW, S5_L * LANE), 0)
    dst = lax.broadcasted_iota(jnp.int32, (S5_CW, S5_L * LANE), 1)
    same_jc = jnp.logical_and(dst // LANE == src // S5_GROUP, dst % S5_GROUP == src % S5_GROUP)
    dst_g = (dst % LANE) // S5_GROUP

    def group(gl, carry):
        place = jnp.where(jnp.logical_and(same_jc, dst_g == gl), 1.0, 0.0).astype(BF16)

        def spread(z):
            return _dot(z.astype(BF16), place).astype(BF16)

        rows_g = pl.ds(pl.multiple_of(gl * S5_GROUP, S5_GROUP), S5_GROUP)
        rows_re = pl.ds(pl.multiple_of(gl * S5_STATE, S5_STATE), S5_STATE)
        rows_im = pl.ds(pl.multiple_of(S5_SH + gl * S5_STATE, S5_STATE), S5_STATE)
        for d in range(2):
            dt = jnp.exp(ldt_ref[d, gl])
            a_r, a_i, k_r, k_i = _s5_discretise(lamc_r_ref[d, gl], lamc_i_ref[d, gl], dt)
            bcol_r, bcol_i = _cmul(k_r, k_i, bc_r_ref[gl], bc_i_ref[gl])
            a_r = jnp.broadcast_to(a_r, (S5_STATE, S5_CW))
            a_i = jnp.broadcast_to(a_i, (S5_STATE, S5_CW))
            up_r, up_i = _lane_power(a_r, a_i, tau)
            dn_r, dn_i = _lane_power(a_r, a_i, S5_L - 1 - tau)
            lag_r, lag_i = (up_r, up_i) if d == 0 else (dn_r, dn_i)
            inp_r, inp_i = (dn_r, dn_i) if d == 0 else (up_r, up_i)
            c_r = _dot_hi(ct_r_ref[d, gl], rep)
            c_i = _dot_hi(ct_i_ref[d, gl], rep)
            wc_r, wc_i = _cmul(lag_r, lag_i, c_r, c_i)
            ar2, ai2, kr2, ki2 = _s5_discretise(lamr_r_ref[d, gl], lamr_i_ref[d, gl], dt)
            btr, bti = _cmul(kr2, ki2, bt_r_ref[gl], bt_i_ref[gl])
            bstack = jnp.where(lane_p < S5_STATE, btr, -bti)
            g = _dot_hi(bstack, jnp.concatenate([wc_r, wc_i], axis=0))
            mi_r, mi_i = _cmul(inp_r, inp_i, _dot_hi(bcol_r, rep), _dot_hi(bcol_i, rep))
            wo_r, wo_i = _cmul(wc_r, wc_i, a_r, a_i)
            wide = spread(jnp.concatenate([g, mi_r, mi_i, wo_r, -wo_i], axis=0))
            o = S5_GROUP
            tcat_ref[d, rows_g, :] = wide[0:o]
            wint_ref[d, rows_re, :] = wide[o:o + S5_STATE]
            wint_ref[d, rows_im, :] = wide[o + S5_STATE:o + 2 * S5_STATE]
            wout_ref[d, rows_re, :] = wide[o + 2 * S5_STATE:o + 3 * S5_STATE]
            wout_ref[d, rows_im, :] = wide[o + 3 * S5_STATE:o + 4 * S5_STATE]
            pr, pi = ar2, ai2
            for _ in range(4):
                pr, pi = _cmul(pr, pi, pr, pi)
            pw = [(jnp.ones_like(pr), jnp.zeros_like(pi))]
            for _ in range(8):
                pw.append(_cmul(pw[-1][0], pw[-1][1], pr, pi))
            rows = [pw[r] if d == 0 else pw[7 - r] for r in range(8)] + [pw[8], pw[1], pw[2], pw[4]]
            for r, (vr, vi) in enumerate(rows):
                pwr_ref[d, gl, r:r + 1, :] = vr
                pwi_ref[d, gl, r:r + 1, :] = vi
            pwr_ref[d, gl, 12:16, :] = jnp.zeros((4, S5_P2), F32)
            pwi_ref[d, gl, 12:16, :] = jnp.zeros((4, S5_P2), F32)
        return carry

    lax.fori_loop(0, S5_GB, group, 0)


def _s5_params(lam_re, lam_im, log_dt, b_re, b_im, c_re, c_im):
    depth, _, ng, ns = lam_re.shape
    dup = lambda z: jnp.concatenate([z, z], axis=-1)
    lamc_r = lam_re.reshape(depth, 2, ng, ns, 1)
    lamc_i = lam_im.reshape(depth, 2, ng, ns, 1)
    lamr_r = dup(lam_re).reshape(depth, 2, ng, 1, 2 * ns)
    lamr_i = dup(lam_im).reshape(depth, 2, ng, 1, 2 * ns)
    ldt = log_dt.reshape(depth, 2, ng, 1, 1)
    bt_r = dup(jnp.swapaxes(b_re, -1, -2))
    bt_i = dup(jnp.swapaxes(b_im, -1, -2))
    ct_r = jnp.swapaxes(c_re, -1, -2)
    ct_i = jnp.swapaxes(c_im, -1, -2)

    nblk = ng // S5_GB
    wide = S5_L * LANE

    def per_dir(shape):
        return pl.BlockSpec((None, 2, S5_GB) + shape, lambda l, gb: (l, 0, gb, 0, 0))

    def shared(shape):
        return pl.BlockSpec((None, S5_GB) + shape, lambda l, gb: (l, gb, 0, 0))

    def op(shape):
        return pl.BlockSpec((None, 2, None) + shape, lambda l, gb: (l, 0, gb, 0, 0))

    tcat, wint, wout, pwr, pwi = pl.pallas_call(
        _s5_param_kernel,
        grid=(depth, nblk),
        in_specs=[per_dir((ns, 1)), per_dir((ns, 1)), per_dir((1, 2 * ns)), per_dir((1, 2 * ns)),
                  per_dir((1, 1)), shared((ns, S5_GROUP)), shared((ns, S5_GROUP)),
                  shared((S5_GROUP, 2 * ns)), shared((S5_GROUP, 2 * ns)),
                  per_dir((ns, S5_GROUP)), per_dir((ns, S5_GROUP))],
        out_specs=[op((LANE, wide)), op((2 * S5_SH, wide)), op((2 * S5_SH, wide)),
                   per_dir((16, S5_P2)), per_dir((16, S5_P2))],
        out_shape=[jax.ShapeDtypeStruct((depth, 2, nblk, LANE, wide), BF16),
                   jax.ShapeDtypeStruct((depth, 2, nblk, 2 * S5_SH, wide), BF16),
                   jax.ShapeDtypeStruct((depth, 2, nblk, 2 * S5_SH, wide), BF16),
                   jax.ShapeDtypeStruct((depth, 2, ng, 16, S5_P2), F32),
                   jax.ShapeDtypeStruct((depth, 2, ng, 16, S5_P2), F32)],
        compiler_params=_cparams("parallel", "parallel"),
        name="s5_params",
    )(lamc_r, lamc_i, lamr_r, lamr_i, ldt, b_re, b_im, bt_r, bt_i, ct_r, ct_i)

    def table(z):
        z = z[..., :S5_STATE].reshape(depth, 2, nblk, S5_GB, 16, S5_STATE)
        return z.transpose(0, 1, 2, 4, 3, 5).reshape(depth, 2, nblk, 16, S5_SH)

    return tcat, wint, wout, table(pwr), table(pwi)


def _s5_mix_kernel(su_ref, t_ref, wint_ref, wout_ref, pwr_ref, pwi_ref, y_ref,
                   u32_ref, ucat_ref, yall_ref, s_ref, x_ref, *, n_ctx):
    d = pl.program_id(1)
    nb, n_tok, gw = su_ref.shape
    nch = n_tok // S5_L
    sh = S5_SH

    @pl.when(d == 0)
    def _():
        for b in range(nb):
            u32_ref[...] = su_ref[b].astype(F32)
            for s in range(S5_L):
                ucat_ref[b * nch:(b + 1) * nch, s * gw:(s + 1) * gw] = (
                    u32_ref[pl.ds(s, nch, stride=S5_L), :].astype(BF16))
        yall_ref[...] = jnp.zeros_like(yall_ref)

    @pl.when(d == 0)
    def _():
        for s in range(S5_L):
            yall_ref[:, s * gw:] += _dot(ucat_ref[:, s * gw:(s + 1) * gw], t_ref[:, 0:(S5_L - s) * gw])

    @pl.when(d == 1)
    def _():
        for s in range(S5_L):
            yall_ref[:, 0:(s + 1) * gw] += _dot(ucat_ref[:, s * gw:(s + 1) * gw], t_ref[:, (S5_L - 1 - s) * gw:])

    s_ref[...] = _dot_nt(ucat_ref[...], wint_ref[...])
    pr, pi = pwr_ref[...], pwi_ref[...]
    tab_r, tab_i = pr[0:8], pi[0:8]
    a8_r, a8_i = pr[8:9], pi[8:9]
    steps = tuple((k, pr[9 + e:10 + e], pi[9 + e:10 + e]) for e, k in enumerate((1, 2, 4)))
    row8 = lax.broadcasted_iota(jnp.int32, (8, sh), 0)
    ntile = nch // 8
    nct = n_ctx // S5_L // 8

    def scan(fwd):
        def shifted(z, k):
            if fwd:
                return jnp.where(row8 >= k, pltpu.roll(z, k, axis=0), 0.0)
            return jnp.where(row8 < 8 - k, pltpu.roll(z, 8 - k, axis=0), 0.0)

        def body(j, carry):
            out = []
            t = j if fwd else jnp.where(j < nct, nct - 1 - j, ntile - 1 - (j - nct))
            for b in range(nb):
                x_r, x_i = carry[b]
                rows = pl.ds(pl.multiple_of(b * nch + t * 8, 8), 8)
                t_r, t_i = s_ref[rows, 0:sh], s_ref[rows, sh:2 * sh]
                for k, k_r, k_i in steps:
                    s_r, s_i = shifted(t_r, k), shifted(t_i, k)
                    t_r, t_i = t_r + k_r * s_r - k_i * s_i, t_i + k_r * s_i + k_i * s_r
                x_ref[rows, 0:sh] = tab_r * x_r - tab_i * x_i + shifted(t_r, 1)
                x_ref[rows, sh:2 * sh] = tab_r * x_i + tab_i * x_r + shifted(t_i, 1)
                e_r, e_i = (t_r[7:8], t_i[7:8]) if fwd else (t_r[0:1], t_i[0:1])
                out.append((a8_r * x_r - a8_i * x_i + e_r, a8_r * x_i + a8_i * x_r + e_i))
            return tuple(out)

        zero = jnp.zeros((1, sh), F32)
        lax.fori_loop(0, ntile, body, tuple((zero, zero) for _ in range(nb)))

    @pl.when(d == 0)
    def _():
        scan(True)

    @pl.when(d == 1)
    def _():
        scan(False)

    yall_ref[...] += _dot(x_ref[...].astype(BF16), wout_ref[...])

    @pl.when(d == 1)
    def _():
        for b in range(nb):
            for s in range(S5_L):
                y_ref[b, pl.ds(s, nch, stride=S5_L), :] = yall_ref[b * nch:(b + 1) * nch, s * gw:(s + 1) * gw]


def _s5_mix(proj3, off_su, tcat, win, wout, pwr, pwi, layer, *, n_ctx):
    nb, n_tok, _ = proj3.shape
    nblk = tcat.shape[2]
    rows = nb * n_tok // S5_L

    def op(shape):
        return pl.BlockSpec((None, None, None) + shape, lambda gb, d: (layer, d, gb, 0, 0))

    return pl.pallas_call(
        functools.partial(_s5_mix_kernel, n_ctx=n_ctx),
        grid=(nblk, 2),
        in_specs=[pl.BlockSpec((nb, n_tok, LANE), lambda gb, d: (0, 0, off_su // LANE + gb)),
                  op((LANE, S5_L * LANE)), op((2 * S5_SH, S5_L * LANE)), op((2 * S5_SH, S5_L * LANE)),
                  op((16, S5_SH)), op((16, S5_SH))],
        out_specs=pl.BlockSpec((nb, n_tok, LANE), lambda gb, d: (0, 0, gb)),
        out_shape=jax.ShapeDtypeStruct((nb, n_tok, S5_WIDTH), F32),
        scratch_shapes=[pltpu.VMEM((n_tok, LANE), F32), pltpu.VMEM((rows, S5_L * LANE), BF16),
                        pltpu.VMEM((rows, S5_L * LANE), F32), pltpu.VMEM((rows, 2 * S5_SH), F32),
                        pltpu.VMEM((rows, 2 * S5_SH), F32)],
        compiler_params=_cparams("parallel", "arbitrary"),
        name="s5_mix",
    )(proj3, tcat, win, wout, pwr, pwi)


def _s5_out_kernel(y_ref, su_ref, d_ref, w_ref, o_ref):
    y = y_ref[...] + su_ref[...].astype(F32) * d_ref[...]
    y = jax.nn.gelu(y)
    o_ref[...] = (y * jax.nn.sigmoid(_dot(y.astype(BF16), w_ref[...]))).astype(BF16)


def _s5_out(y, proj, off_su, d_skip, w_glu, *, tm):
    m = y.shape[0]
    row = pl.BlockSpec((tm, S5_WIDTH), lambda i: (i, 0))
    return pl.pallas_call(
        _s5_out_kernel,
        grid=(m // tm,),
        in_specs=[row,
                  pl.BlockSpec((tm, S5_WIDTH), lambda i: (i, off_su // S5_WIDTH)),
                  pl.BlockSpec((1, S5_WIDTH), lambda i: (0, 0)),
                  pl.BlockSpec((S5_WIDTH, S5_WIDTH), lambda i: (0, 0))],
        out_specs=row,
        out_shape=jax.ShapeDtypeStruct((m, S5_WIDTH), BF16),
        compiler_params=_cparams("parallel"),
        name="s5_out",
    )(y, proj, d_skip.reshape(1, S5_WIDTH), w_glu)


def _rope_tables(n_ctx, n_lat):
    rows = n_lat // GRID_W
    r = jnp.repeat(jnp.arange(rows), GRID_W).astype(F32)
    c = jnp.tile(jnp.arange(GRID_W), rows).astype(F32)
    n_freq = ATTN_HEAD_DIM // 4
    inv = ROPE_THETA ** (-jnp.arange(n_freq, dtype=F32) / n_freq)
    ang = jnp.concatenate([r[:, None] * inv, c[:, None] * inv], -1)
    cos, sin = jnp.cos(ang), jnp.sin(ang)
    cos2 = jnp.repeat(cos, 2, axis=-1)
    sin2 = jnp.stack([-sin, sin], -1).reshape(n_lat, ATTN_HEAD_DIM)
    cos2 = jnp.concatenate([jnp.ones((n_ctx, ATTN_HEAD_DIM), F32), cos2], 0)
    sin2 = jnp.concatenate([jnp.zeros((n_ctx, ATTN_HEAD_DIM), F32), sin2], 0)
    return cos2, sin2


def _qk_prep_kernel(q_ref, k_ref, cos_ref, sin_ref, qw_ref, kw_ref, qo_ref, ko_ref):
    cos = cos_ref[...]
    sin = sin_ref[...]
    even = (lax.broadcasted_iota(jnp.int32, cos.shape, 1) % 2) == 0

    def norm_rope(x, w):
        xn = x * lax.rsqrt(jnp.mean(x * x, -1, keepdims=True) + EPS) * w
        partner = jnp.where(even, pltpu.roll(xn, ATTN_HEAD_DIM - 1, axis=1), pltpu.roll(xn, 1, axis=1))
        return xn * cos + partner * sin

    for h in range(ATTN_Q_HEADS):
        sl = slice(h * ATTN_HEAD_DIM, (h + 1) * ATTN_HEAD_DIM)
        qo_ref[:, sl] = (norm_rope(q_ref[:, sl].astype(F32), qw_ref[...]) * (ATTN_HEAD_DIM ** -0.5)).astype(BF16)
    for h in range(ATTN_KV_HEADS):
        sl = slice(h * ATTN_HEAD_DIM, (h + 1) * ATTN_HEAD_DIM)
        ko_ref[:, sl] = norm_rope(k_ref[:, sl].astype(F32), kw_ref[...]).astype(BF16)


def _qk_prep(proj, offs, cos2, sin2, q_w, k_w, *, tm, tpb):
    m = proj.shape[0]
    oq, ok = offs["aq"], offs["ak"]
    vec = pl.BlockSpec((1, ATTN_HEAD_DIM), lambda i: (0, 0))
    tab = pl.BlockSpec((tm, ATTN_HEAD_DIM), lambda i: (i % tpb, 0))
    return pl.pallas_call(
        _qk_prep_kernel,
        grid=(m // tm,),
        in_specs=[pl.BlockSpec((tm, ATTN_Q), lambda i: (i, oq // ATTN_Q)),
                  pl.BlockSpec((tm, ATTN_KV), lambda i: (i, ok // ATTN_KV)),
                  tab, tab, vec, vec],
        out_specs=[pl.BlockSpec((tm, ATTN_Q), lambda i: (i, 0)),
                   pl.BlockSpec((tm, ATTN_KV), lambda i: (i, 0))],
        out_shape=[jax.ShapeDtypeStruct((m, ATTN_Q), BF16),
                   jax.ShapeDtypeStruct((m, ATTN_KV), BF16)],
        compiler_params=_cparams("parallel"),
        name="qk_prep",
    )(proj, proj, cos2, sin2, q_w.reshape(1, -1), k_w.reshape(1, -1))


def _attn_kernel(q_ref, k_ref, v_ref, o_ref, *, n_ctx, tq):
    def attend(kk, vv):
        for g in range(ATTN_GROUP):
            sl = slice(g * ATTN_HEAD_DIM, (g + 1) * ATTN_HEAD_DIM)
            s = _dot_nt(q_ref[:, sl], kk)
            p = jnp.exp(s - jnp.max(s, -1, keepdims=True))
            l = jnp.sum(p, -1, keepdims=True)
            o_ref[:, sl] = (_dot(p.astype(BF16), vv) / l).astype(BF16)

    is_ctx = pl.program_id(2) < n_ctx // tq

    @pl.when(is_ctx)
    def _():
        attend(k_ref[0:n_ctx, :], v_ref[0:n_ctx, :])

    @pl.when(jnp.logical_not(is_ctx))
    def _():
        attend(k_ref[...], v_ref[...])


def _attention(q3, k3, proj3, off_v, *, n_ctx):
    nb, n_tok, _ = q3.shape
    tq = 256
    qw = ATTN_GROUP * ATTN_HEAD_DIM
    vb0 = off_v // ATTN_HEAD_DIM
    return pl.pallas_call(
        functools.partial(_attn_kernel, n_ctx=n_ctx, tq=tq),
        grid=(nb, ATTN_KV_HEADS, n_tok // tq),
        in_specs=[pl.BlockSpec((None, tq, qw), lambda b, h, i: (b, i, h)),
                  pl.BlockSpec((None, n_tok, ATTN_HEAD_DIM), lambda b, h, i: (b, 0, h)),
                  pl.BlockSpec((None, n_tok, ATTN_HEAD_DIM), lambda b, h, i: (b, 0, vb0 + h))],
        out_specs=pl.BlockSpec((None, tq, qw), lambda b, h, i: (b, i, h)),
        out_shape=jax.ShapeDtypeStruct((nb, n_tok, ATTN_Q), BF16),
        compiler_params=_cparams("parallel", "parallel", "arbitrary"),
        name="attention",
    )(q3, k3, proj3)


def _in_layout(d):
    segs = [("gq", GLA_QK), ("gk", GLA_QK), ("gv", GLA_V), ("gr", GLA_V), ("su", S5_WIDTH), ("glr", GLR_PAD),
            ("aq", ATTN_Q), ("ak", ATTN_KV), ("av", ATTN_KV), ("bg", 3 * d)]
    offs, o = {}, 0
    for name, w in segs:
        offs[name] = o
        o += w
    return segs, offs, o


def _pack_w_in(w_in, d):
    ref_order = [("gq", GLA_QK), ("gk", GLA_QK), ("gv", GLA_V), ("gr", GLA_V), ("glr", GLA_GATE_RANK),
                 ("su", S5_WIDTH), ("aq", ATTN_Q), ("ak", ATTN_KV), ("av", ATTN_KV), ("bg", 3 * d)]
    src, o = {}, 0
    for name, w in ref_order:
        src[name] = (o, w)
        o += w
    segs, _, _ = _in_layout(d)
    runs = []
    for name, w in segs:
        lo, sw = src[name]
        if runs and runs[-1][1] == lo and runs[-1][2] == 0:
            runs[-1][1] = lo + sw
        else:
            runs.append([lo, lo + sw, 0])
        runs[-1][2] = w - sw
    parts = []
    for lo, hi, pad in runs:
        part = w_in[:, :, lo:hi].astype(BF16)
        if pad:
            part = jnp.pad(part, ((0, 0), (0, 0), (0, pad)))
        parts.append(part)
    return jnp.concatenate(parts, axis=-1)


def kernel(x, c, ctx, c_ctx, w_ada, b_ada, w_in, w_gla_gate, b_gla_gate, gla_norm_w, s5_lam_re, s5_lam_im,
           s5_log_dt, s5_b_re, s5_b_im, s5_c_re, s5_c_im, s5_d, w_s5_glu, q_norm_w, k_norm_w, w_proj_gla,
           w_proj_s5, w_proj_attn, w_out, ln1_w, ln1_b, ln2_w, ln2_b, w_ffn_in, w_ffn_out):
    nb, n_lat, d = x.shape
    n_ctx = ctx.shape[1]
    n_tok = n_ctx + n_lat
    depth = w_in.shape[0]
    alpha = (2 * depth) ** 0.25
    m = nb * n_tok
    tm = _pick(n_tok, (768, 256))
    tpb = n_tok // tm
    assert n_ctx <= tm and n_ctx % 256 == 0 and n_lat % 256 == 0 and nb + 1 <= 8
    assert (n_ctx // S5_L) % 8 == 0 and (n_tok // S5_L) % 8 == 0 and S5_GB * S5_GROUP == LANE
    tile = dict(tm=tm, tpb=tpb, nb=nb, n_ctx=n_ctx)
    tm_in = _pick(n_tok, (1152, 768, 256))
    tile_in = dict(tm=tm_in, tpb=n_tok // tm_in, nb=nb, n_ctx=n_ctx)

    _, offs, _ = _in_layout(d)
    for name, width in (("gv", GLA_DV), ("gr", GLA_DV), ("aq", ATTN_GROUP * ATTN_HEAD_DIM), ("su", S5_WIDTH),
                        ("ak", ATTN_KV), ("bg", 2 * LANE), ("gq", LANE), ("gk", LANE), ("av", LANE), ("glr", LANE)):
        assert offs[name] % width == 0, (name, offs[name])
    w_in_p = _pack_w_in(w_in, d)
    w_gate_p = jnp.pad(w_gla_gate, ((0, 0), (0, 0), (0, LANE - GLA_GATE_RANK), (0, 0)))
    cos2, sin2 = _rope_tables(n_ctx, n_lat)
    cc = jnp.concatenate([c, c_ctx[None], jnp.zeros((8 - nb - 1, d), F32)], 0)
    mod = _ada_table(cc, w_ada, b_ada)
    s5_ops = _s5_params(s5_lam_re, s5_lam_im, s5_log_dt, s5_b_re, s5_b_im, s5_c_re, s5_c_im)

    xs = jnp.concatenate([ctx, x], axis=1).reshape(m, d)
    for l in range(depth):
        mod_l = mod[l]
        proj = _in_proj(xs, mod_l, w_in_p[l], **tile_in)
        proj3 = proj.reshape(nb, n_tok, -1)

        o_gla = _gla(proj3, offs, w_gate_p[l], b_gla_gate[l], gla_norm_w[l], n_ctx=n_ctx).reshape(m, GLA_V)

        y_s5 = _s5_mix(proj3, offs["su"], *s5_ops, l, n_ctx=n_ctx).reshape(m, S5_WIDTH)
        o_s5 = _s5_out(y_s5, proj, offs["su"], s5_d[l], w_s5_glu[l].astype(BF16), tm=tm)

        q_r, k_r = _qk_prep(proj, offs, cos2, sin2, q_norm_w[l], k_norm_w[l], tm=tm, tpb=tpb)
        o_attn = _attention(q_r.reshape(nb, n_tok, ATTN_Q), k_r.reshape(nb, n_tok, ATTN_KV),
                            proj3, offs["av"], n_ctx=n_ctx).reshape(m, ATTN_Q)

        xs = _mix_out(o_gla, o_s5, o_attn, w_proj_gla[l].astype(BF16), w_proj_s5[l].astype(BF16),
                      w_proj_attn[l].astype(BF16), w_out[l].astype(BF16), proj, offs["bg"], xs, mod_l,
                      ln1_w[l], ln1_b[l], alpha=alpha, **tile)
        xs = _ffn(xs, mod_l, w_ffn_in[l].astype(BF16), w_ffn_out[l].astype(BF16), ln2_w[l], ln2_b[l],
                  alpha=alpha, **tile)
    return xs.reshape(nb, n_tok, d)[:, n_ctx:]
```
